```python
import jax
import jax.numpy as jnp
from jax import lax
import numpy as np

D_MODEL = 1024
BATCH = 8
SEQ = 2048
DEPTH = 4

GRID_W = 64
CTX_LEN = 256
CHUNK = 64
N_MIXERS = 2
RET_HEADS = 4
RET_QK = D_MODEL
RET_V = 2 * D_MODEL
RET_DK = RET_QK // RET_HEADS
RET_DV = RET_V // RET_HEADS
RET_IN = 2 * RET_QK + 2 * RET_V
GLA_HEADS = 4
GLA_K = D_MODEL // 2
GLA_V = D_MODEL
GLA_DK = GLA_K // GLA_HEADS
GLA_DV = GLA_V // GLA_HEADS
GLA_GATE_RANK = 16
GLA_TAU = 16.0
GLA_IN = 2 * GLA_K + 2 * GLA_V + 2 * GLA_GATE_RANK
D_FF = ((8 * D_MODEL // 3 + 255) // 256) * 256
N_EXPERTS = 8
TOP_K = 2
D_FF_EXPERT = 7 * D_MODEL // 2
ROPE_BASE = 10000.0
EPS = 1e-6
N_RET = (DEPTH + 1) // 2
N_GLA = DEPTH // 2

kernel_name = 'hybrid_retention_gla_moe_prefix_dit'


def rmsnorm(x, g):
    xf = x.astype(jnp.float32)
    y = xf * lax.rsqrt(jnp.mean(xf * xf, axis=-1, keepdims=True) + EPS)
    return (y * g.astype(jnp.float32)).astype(x.dtype)


def flip(t):
    return jnp.flip(t, axis=1)


def to_chunks(t):
    b, l, h, d = t.shape
    return t.reshape(b, l // CHUNK, CHUNK, h, d).transpose(1, 0, 3, 2, 4)


def from_chunks(t):
    n, b, h, c, d = t.shape
    return t.transpose(1, 0, 3, 2, 4).reshape(b, n * c, h, d)


def rope_1d(t, pos):
    half = t.shape[-1] // 2
    freqs = ROPE_BASE ** (-jnp.arange(half, dtype=jnp.float32) / half)
    ang = pos[:, None] * freqs[None, :]
    cos = jnp.cos(ang)[None, :, None, :].astype(t.dtype)
    sin = jnp.sin(ang)[None, :, None, :].astype(t.dtype)
    t1, t2 = t[..., :half], t[..., half:]
    return jnp.concatenate([t1 * cos - t2 * sin, t1 * sin + t2 * cos], axis=-1)


def rope_axial(t):
    length = t.shape[1]
    rows = length // GRID_W
    row = jnp.broadcast_to(jnp.arange(rows, dtype=jnp.float32)[:, None], (rows, GRID_W)).reshape(length)
    col = jnp.broadcast_to(jnp.arange(GRID_W, dtype=jnp.float32)[None, :], (rows, GRID_W)).reshape(length)
    half = t.shape[-1] // 2
    return jnp.concatenate([rope_1d(t[..., :half], row), rope_1d(t[..., half:], col)], axis=-1)


def retention_scan(q, k, v, log_gamma, state0):
    pos = jnp.arange(CHUNK, dtype=jnp.float32)
    lg = log_gamma.astype(jnp.float32)
    diff = pos[:, None] - pos[None, :]
    dmat = jnp.where(diff >= 0, jnp.exp(lg[:, None, None] * jnp.maximum(diff, 0.0)), 0.0)
    q_decay = jnp.exp(lg[:, None] * (pos + 1.0))[:, :, None]
    k_decay = jnp.exp(lg[:, None] * (CHUNK - 1.0 - pos))[:, :, None]
    chunk_decay = jnp.exp(lg * CHUNK)[:, None, None]

    def step(s, inp):
        qc, kc, vc = inp
        scores = jnp.einsum('bhid,bhjd->bhij', qc, kc) * dmat
        o = jnp.einsum('bhij,bhjv->bhiv', scores, vc) + jnp.einsum('bhid,bhdv->bhiv', qc * q_decay, s)
        s = s * chunk_decay + jnp.einsum('bhjd,bhjv->bhdv', kc * k_decay, vc)
        return s, o

    s, o = lax.scan(step, state0, (to_chunks(q), to_chunks(k), to_chunks(v)))
    return from_chunks(o), s


def retention_state(k, v, log_gamma):
    length = k.shape[1]
    t = jnp.arange(length, dtype=jnp.float32)
    w = jnp.exp(log_gamma.astype(jnp.float32)[None, :] * (length - 1.0 - t)[:, None])
    return jnp.einsum('blhd,blhv->bhdv', k * w[None, :, :, None], v).astype(jnp.float32)


def gla_scan(q, k, v, log_a, state0):
    causal = jnp.tril(jnp.ones((CHUNK, CHUNK), dtype=bool))[:, :, None]

    def step(s, inp):
        qc, kc, vc, gc = inp
        b = jnp.cumsum(gc, axis=2)
        rel = jnp.where(causal, b[:, :, :, None, :] - b[:, :, None, :, :], -jnp.inf)
        scores = jnp.einsum('bhjd,bhid,bhjid->bhji', qc, kc, jnp.exp(rel))
        o = jnp.einsum('bhji,bhiv->bhjv', scores, vc) + jnp.einsum('bhjd,bhdv->bhjv', qc * jnp.exp(b), s)
        b_last = b[:, :, -1:, :]
        s = s * jnp.exp(b_last)[:, :, 0, :, None] + jnp.einsum('bhid,bhiv->bhdv', kc * jnp.exp(b_last - b), vc)
        return s, o

    s, o = lax.scan(step, state0, (to_chunks(q), to_chunks(k), to_chunks(v), to_chunks(log_a)))
    return from_chunks(o), s


def gla_state(k, v, log_a):
    b = jnp.cumsum(log_a, axis=1)
    return jnp.einsum('blhd,blhv->bhdv', k * jnp.exp(b[:, -1:] - b), v).astype(jnp.float32)


def retention_mixer(h_ctx, h_lat, w_in, log_decay, gn_w, gn_b, w_out, with_ctx_out):
    batch = h_lat.shape[0]
    log_gamma = -jnp.exp(log_decay.astype(jnp.float32))
    qs = RET_QK + RET_V

    def query_side(p):
        b, l, _ = p.shape
        return p[..., :RET_QK].reshape(b, l, RET_HEADS, RET_DK), p[..., RET_QK:]

    def state_side(p):
        b, l, _ = p.shape
        k = p[..., :RET_QK].reshape(b, l, RET_HEADS, RET_DK) * (RET_DK ** -0.5)
        return k, p[..., RET_QK:].reshape(b, l, RET_HEADS, RET_DV)

    def finish(o, g):
        b, l = o.shape[:2]
        mu = jnp.mean(o, axis=-1, keepdims=True)
        var = jnp.mean(jnp.square(o - mu), axis=-1, keepdims=True)
        o = ((o - mu) * lax.rsqrt(var + EPS)).reshape(b, l, RET_V)
        o = (o * gn_w.astype(jnp.float32) + gn_b.astype(jnp.float32)).astype(g.dtype)
        return (o * jax.nn.silu(g)) @ w_out

    zeros = jnp.zeros((batch, RET_HEADS, RET_DK, RET_DV), jnp.float32)
    p_lat = h_lat @ w_in
    ql, gl = query_side(p_lat[..., :qs])
    kl, vl = state_side(p_lat[..., qs:])
    ql, kl = rope_axial(ql), rope_axial(kl)
    if with_ctx_out:
        p_ctx = h_ctx @ w_in
        qc, gc = query_side(p_ctx[..., :qs])
        kc, vc = state_side(p_ctx[..., qs:])
        oc_f, s_f = retention_scan(qc, kc, vc, log_gamma[0], zeros)
        oc_b, s_b = retention_scan(flip(qc), flip(kc), flip(vc), log_gamma[1], zeros)
        out_ctx = finish(oc_f + flip(oc_b), gc)
    else:
        kc, vc = state_side(h_ctx @ w_in[:, qs:])
        s_f = retention_state(kc, vc, log_gamma[0])
        s_b = retention_state(flip(kc), flip(vc), log_gamma[1])
        out_ctx = None
    ol_f, _ = retention_scan(ql, kl, vl, log_gamma[0], s_f)
    ol_b, _ = retention_scan(flip(ql), flip(kl), flip(vl), log_gamma[1], s_b)
    out_lat = finish(ol_f + flip(ol_b), gl)
    return out_lat, out_ctx


def gla_log_gate(a, w_up, b):
    z = (a @ w_up + b).astype(jnp.float32)
    return (jax.nn.log_sigmoid(z) / GLA_TAU).reshape(a.shape[0], a.shape[1], GLA_HEADS, GLA_DK)


def gla_mixer(h_ctx, h_lat, w_in, w_gate_up, b_gate, norm_g, w_out, with_ctx_out):
    batch = h_lat.shape[0]
    qs = GLA_K + GLA_V

    def query_side(p):
        b, l, _ = p.shape
        q = p[..., :GLA_K].reshape(b, l, GLA_HEADS, GLA_DK) * (GLA_DK ** -0.5)
        return q, p[..., GLA_K:]

    def state_side(p):
        b, l, _ = p.shape
        k = p[..., :GLA_K].reshape(b, l, GLA_HEADS, GLA_DK)
        v = p[..., GLA_K:GLA_K + GLA_V].reshape(b, l, GLA_HEADS, GLA_DV)
        a_f = p[..., GLA_K + GLA_V:GLA_K + GLA_V + GLA_GATE_RANK]
        a_b = p[..., GLA_K + GLA_V + GLA_GATE_RANK:]
        return k, v, gla_log_gate(a_f, w_gate_up[0], b_gate[0]), gla_log_gate(a_b, w_gate_up[1], b_gate[1])

    def finish(o, r):
        b, l = o.shape[:2]
        o = o * lax.rsqrt(jnp.mean(o * o, axis=-1, keepdims=True) + EPS)
        o = (o.reshape(b, l, GLA_V) * norm_g.astype(jnp.float32)).astype(r.dtype)
        return (o * jax.nn.silu(r)) @ w_out

    zeros = jnp.zeros((batch, GLA_HEADS, GLA_DK, GLA_DV), jnp.float32)
    p_lat = h_lat @ w_in
    ql, rl = query_side(p_lat[..., :qs])
    kl, vl, laf_l, lab_l = state_side(p_lat[..., qs:])
    if with_ctx_out:
        p_ctx = h_ctx @ w_in
        qc, rc = query_side(p_ctx[..., :qs])
        kc, vc, laf_c, lab_c = state_side(p_ctx[..., qs:])
        oc_f, s_f = gla_scan(qc, kc, vc, laf_c, zeros)
        oc_b, s_b = gla_scan(flip(qc), flip(kc), flip(vc), flip(lab_c), zeros)
        out_ctx = finish(oc_f + flip(oc_b), rc)
    else:
        kc, vc, laf_c, lab_c = state_side(h_ctx @ w_in[:, qs:])
        s_f = gla_state(kc, vc, laf_c)
        s_b = gla_state(flip(kc), flip(vc), flip(lab_c))
        out_ctx = None
    ol_f, _ = gla_scan(ql, kl, vl, laf_l, s_f)
    ol_b, _ = gla_scan(flip(ql), flip(kl), flip(vl), flip(lab_l), s_b)
    out_lat = finish(ol_f + flip(ol_b), rl)
    return out_lat, out_ctx


def swiglu(h, w_gate, w_up, w_down):
    return (jax.nn.silu(h @ w_gate) * (h @ w_up)) @ w_down


def moe_swiglu(h, w_router, w_gate, w_up, w_down):
    logits = (h @ w_router).astype(jnp.float32)
    top_vals, top_idx = lax.top_k(logits, TOP_K)
    top_w = jax.nn.softmax(top_vals, axis=-1)
    gates = jnp.sum(jax.nn.one_hot(top_idx, N_EXPERTS, dtype=jnp.float32) * top_w[..., None], axis=-2)
    out = jnp.zeros_like(h)
    for e in range(N_EXPERTS):
        out = out + gates[..., e:e + 1].astype(h.dtype) * swiglu(h, w_gate[e], w_up[e], w_down[e])
    return out


def setup_inputs(seed: int = 0) -> dict:
    key = jax.random.key(seed)
    keys = iter(jax.random.split(key, 32))
    f32 = jnp.float32
    d = D_MODEL

    def nrm(shape, scale):
        return jax.random.normal(next(keys), shape, f32) * scale

    base_decay = jnp.log(-jnp.log1p(-(2.0 ** (-5.0 - jnp.arange(RET_HEADS, dtype=f32)))))
    return {
        'x': nrm((BATCH, SEQ, d), 1.0),
        'c': nrm((BATCH, d), 1.0),
        'ctx': nrm((BATCH, CTX_LEN, d), 1.0),
        'c_ctx': nrm((d,), 1.0),
        'ada_w': nrm((DEPTH, d, 6 * d), 0.5 * d ** -0.5),
        'ada_b': nrm((DEPTH, 6 * d), 0.02),
        'norm_mix_g': 1.0 + nrm((DEPTH, d), 0.02),
        'norm_ffn_g': 1.0 + nrm((DEPTH, d), 0.02),
        'final_g': 1.0 + nrm((d,), 0.02),
        'ret_w_in': nrm((N_RET, d, RET_IN), d ** -0.5),
        'ret_log_decay': base_decay[None, None, :] + nrm((N_RET, 2, RET_HEADS), 0.1),
        'ret_gn_w': 1.0 + nrm((N_RET, RET_V), 0.02),
        'ret_gn_b': nrm((N_RET, RET_V), 0.02),
        'ret_w_out': nrm((N_RET, RET_V, d), RET_V ** -0.5),
        'gla_w_in': nrm((N_GLA, d, GLA_IN), d ** -0.5),
        'gla_w_gate_up': nrm((N_GLA, 2, GLA_GATE_RANK, GLA_K), GLA_GATE_RANK ** -0.5),
        'gla_b_gate': nrm((N_GLA, 2, GLA_K), 0.1),
        'gla_norm_g': 1.0 + nrm((N_GLA, GLA_V), 0.02),
        'gla_w_out': nrm((N_GLA, GLA_V, d), GLA_V ** -0.5),
        'ffn_w_gate': nrm((N_RET, d, D_FF), d ** -0.5),
        'ffn_w_up': nrm((N_RET, d, D_FF), d ** -0.5),
        'ffn_w_down': nrm((N_RET, D_FF, d), D_FF ** -0.5),
        'moe_w_router': nrm((N_GLA, d, N_EXPERTS), d ** -0.5),
        'moe_w_gate': nrm((N_GLA, N_EXPERTS, d, D_FF_EXPERT), d ** -0.5),
        'moe_w_up': nrm((N_GLA, N_EXPERTS, d, D_FF_EXPERT), d ** -0.5),
        'moe_w_down': nrm((N_GLA, N_EXPERTS, D_FF_EXPERT, d), D_FF_EXPERT ** -0.5),
    }


def reference(x, c, ctx, c_ctx, ada_w, ada_b, norm_mix_g, norm_ffn_g, final_g,
              ret_w_in, ret_log_decay, ret_gn_w, ret_gn_b, ret_w_out,
              gla_w_in, gla_w_gate_up, gla_b_gate, gla_norm_g, gla_w_out,
              ffn_w_gate, ffn_w_up, ffn_w_down,
              moe_w_router, moe_w_gate, moe_w_up, moe_w_down):
    lat, cx = x, ctx
    silu_c = jax.nn.silu(c)
    silu_cc = jax.nn.silu(c_ctx)
    for i in range(DEPTH):
        last = i == DEPTH - 1
        j = i // N_MIXERS
        mod_l = (silu_c @ ada_w[i] + ada_b[i])[:, None, :]
        mod_c = (silu_cc @ ada_w[i] + ada_b[i])[None, None, :]
        sh1_l, sc1_l, g1_l, sh2_l, sc2_l, g2_l = jnp.split(mod_l, 6, axis=-1)
        sh1_c, sc1_c, g1_c, sh2_c, sc2_c, g2_c = jnp.split(mod_c, 6, axis=-1)

        h_l = rmsnorm(lat, norm_mix_g[i]) * (1.0 + sc1_l) + sh1_l
        h_c = rmsnorm(cx, norm_mix_g[i]) * (1.0 + sc1_c) + sh1_c
        if i % N_MIXERS == 0:
            mix_l, mix_c = retention_mixer(h_c, h_l, ret_w_in[j], ret_log_decay[j], ret_gn_w[j],
                                           ret_gn_b[j], ret_w_out[j], not last)
        else:
            mix_l, mix_c = gla_mixer(h_c, h_l, gla_w_in[j], gla_w_gate_up[j], gla_b_gate[j],
                                     gla_norm_g[j], gla_w_out[j], not last)
        lat = lat + g1_l * mix_l.astype(lat.dtype)
        if not last:
            cx = cx + g1_c * mix_c.astype(cx.dtype)

        h_l = rmsnorm(lat, norm_ffn_g[i]) * (1.0 + sc2_l) + sh2_l
        if last:
            n_c = 0
            h_all = h_l
        else:
            n_c = cx.shape[1]
            h_c = rmsnorm(cx, norm_ffn_g[i]) * (1.0 + sc2_c) + sh2_c
            h_all = jnp.concatenate([h_c, h_l], axis=1)
        if i % 2 == 0:
            f = swiglu(h_all, ffn_w_gate[j], ffn_w_up[j], ffn_w_down[j])
        else:
            f = moe_swiglu(h_all, moe_w_router[j], moe_w_gate[j], moe_w_up[j], moe_w_down[j])
        lat = lat + g2_l * f[:, n_c:].astype(lat.dtype)
        if not last:
            cx = cx + g2_c * f[:, :n_c].astype(cx.dtype)
    return rmsnorm(lat, final_g)
```

```python
import functools

import jax
import jax.numpy as jnp
from jax import lax
from jax.experimental import pallas as pl
from jax.experimental.pallas import tpu as pltpu

D_MODEL = 1024
GRID_W = 64
CTX_LEN = 256
DEPTH = 4
RET_HEADS = 4
RET_DK = 256
RET_DV = 512
RET_QK = 1024
RET_V = 2048
GLA_HEADS = 4
GLA_DK = 128
GLA_DV = 256
GLA_K = 512
GLA_V = 1024
GLA_GATE_RANK = 16
GLA_TAU = 16.0
D_FF = 2816
N_EXPERTS = 8
D_FF_EXPERT = 3584
ROPE_BASE = 10000.0
EPS = 1e-6

BF16 = jnp.bfloat16
F32 = jnp.float32

SCAN_ROWS = 256
GLA_SUB = 64
GLA_BLK = 16
VMEM_LIMIT = 48 * 1024 * 1024


def _cparams(sem):
    return pltpu.CompilerParams(dimension_semantics=sem, vmem_limit_bytes=VMEM_LIMIT)


def _dot(a, b):
    return jnp.dot(a, b, preferred_element_type=F32)


def _dot_nt(a, b):
    return lax.dot_general(a, b, (((1,), (1,)), ((), ())), preferred_element_type=F32)


def _dot_tn(a, b):
    return lax.dot_general(a, b, (((0,), (0,)), ((), ())), preferred_element_type=F32)


def _silu(x):
    return x * jax.nn.sigmoid(x)


def _norm_mod(x, gain, mod_ref, shift_i, scale_i):
    ms = jnp.mean(x * x, axis=-1, keepdims=True)
    y = x * lax.rsqrt(ms + EPS) * gain
    return y * (1.0 + mod_ref[scale_i]) + mod_ref[shift_i]


class _Rows:
    def __init__(self, batch, seq, tm, lat_only):
        self.tm = tm
        self.nct = batch * CTX_LEN // tm
        self.tpb = seq // tm
        self.start = self.nct if lat_only else 0
        self.n = batch * seq // tm + (0 if lat_only else self.nct)
        self.batch = batch

    def blk(self, base_tile=0, col=0):
        off = self.start - base_tile
        return lambda t, *_: (t + off, col)

    def mod_blk(self):
        start, nct, tpb, batch = self.start, self.nct, self.tpb, self.batch

        def im(t, *_):
            tg = t + start
            return (jnp.where(tg < nct, batch, (tg - nct) // tpb), 0, 0, 0)
        return im

    def pos_blk(self):
        start, nct, tpb = self.start, self.nct, self.tpb

        def im(t, *_):
            tg = t + start
            return (jnp.where(tg < nct, 0, 1 + (tg - nct) % tpb), 0)
        return im


def _mod_spec(rows):
    return pl.BlockSpec((None, 6, 1, D_MODEL), rows.mod_blk())


def _mod_kernel(c_ref, w_ref, b_ref, o_ref):
    s = _silu(c_ref[...]).astype(BF16)
    o_ref[...] = _dot(s, w_ref[...].astype(BF16)) + b_ref[...]


def _modulation(c, c_ctx, ada_w, ada_b):
    batch = c.shape[0]
    nrow = 16
    cc = jnp.concatenate([c, c_ctx[None], jnp.zeros((nrow - batch - 1, D_MODEL), F32)], axis=0)
    out = pl.pallas_call(
        _mod_kernel,
        out_shape=jax.ShapeDtypeStruct((DEPTH, 6, nrow, D_MODEL), F32),
        grid=(DEPTH, 6),
        in_specs=[
            pl.BlockSpec((nrow, D_MODEL), lambda i, k: (0, 0)),
            pl.BlockSpec((None, D_MODEL, D_MODEL), lambda i, k: (i, 0, k)),
            pl.BlockSpec((None, None, 1, D_MODEL), lambda i, k: (i, k, 0, 0)),
        ],
        out_specs=pl.BlockSpec((None, None, nrow, D_MODEL), lambda i, k: (i, k, 0, 0)),
        compiler_params=_cparams(("arbitrary", "arbitrary")),
        name="adaln_mod",
    )(cc, ada_w, ada_b.reshape(DEPTH, 6, 1, D_MODEL))
    return out.transpose(0, 2, 1, 3)[:, :, :, None, :]


def _ret_proj_kernel(x_ref, mod_ref, gain_ref, w_ref, cs_ref, cos_ref, sin_ref, o_ref, h_scr):
    j = pl.program_id(1)

    @pl.when(j == 0)
    def _():
        h_scr[...] = _norm_mod(x_ref[...], gain_ref[...], mod_ref, 0, 1).astype(BF16)

    acc = _dot(h_scr[...], w_ref[...]) * cs_ref[...]
    is_rope = jnp.logical_or(j == 0, j == 3)

    @pl.when(is_rope)
    def _():
        for blk in range(D_MODEL // 128):
            lo = blk * 128
            tl = (blk % 2) * 128
            a = acc[:, lo:lo + 128]
            rot = pltpu.roll(a, 64, 1)
            o_ref[:, lo:lo + 128] = (a * cos_ref[:, tl:tl + 128]
                                     + rot * sin_ref[:, tl:tl + 128]).astype(BF16)

    @pl.when(jnp.logical_not(is_rope))
    def _():
        o_ref[...] = acc.astype(BF16)


def _rope_tables(seq, tm):
    half = RET_DK // 4
    freqs = ROPE_BASE ** (-jnp.arange(half, dtype=F32) / half)
    rows = seq // GRID_W
    row = jnp.broadcast_to(jnp.arange(rows, dtype=F32)[:, None], (rows, GRID_W)).reshape(seq)
    col = jnp.broadcast_to(jnp.arange(GRID_W, dtype=F32)[None, :], (rows, GRID_W)).reshape(seq)
    ar = row[:, None] * freqs[None, :]
    ac = col[:, None] * freqs[None, :]
    cos = jnp.concatenate([jnp.cos(ar), jnp.cos(ar), jnp.cos(ac), jnp.cos(ac)], axis=-1)
    sin = jnp.concatenate([-jnp.sin(ar), jnp.sin(ar), -jnp.sin(ac), jnp.sin(ac)], axis=-1)
    cos = jnp.concatenate([jnp.ones((tm, RET_DK), F32), cos], axis=0)
    sin = jnp.concatenate([jnp.zeros((tm, RET_DK), F32), sin], axis=0)
    return cos, sin


def _ret_proj(xs, mod, gain, w_bf, colscale, cos, sin, rows):
    tm, tn = rows.tm, 1024
    nout = w_bf.shape[1]
    return pl.pallas_call(
        _ret_proj_kernel,
        out_shape=jax.ShapeDtypeStruct((xs.shape[0], nout), BF16),
        grid=(rows.n, nout // tn),
        in_specs=[
            pl.BlockSpec((tm, D_MODEL), rows.blk()),
            _mod_spec(rows),
            pl.BlockSpec((1, D_MODEL), lambda t, j: (0, 0)),
            pl.BlockSpec((D_MODEL, tn), lambda t, j: (0, j)),
            pl.BlockSpec((1, tn), lambda t, j: (0, j)),
            pl.BlockSpec((tm, RET_DK), rows.pos_blk()),
            pl.BlockSpec((tm, RET_DK), rows.pos_blk()),
        ],
        out_specs=pl.BlockSpec((tm, tn), lambda t, j: (t + rows.start, j)),
        scratch_shapes=[pltpu.VMEM((tm, D_MODEL), BF16)],
        compiler_params=_cparams(("arbitrary", "arbitrary")),
        name="ret_proj",
    )(xs, mod, gain, w_bf, colscale, cos, sin)


def _log_sigmoid(z):
    return jnp.minimum(z, 0.0) - jnp.log1p(jnp.exp(-jnp.abs(z)))


def _gla_proj_kernel(x_ref, mod_ref, gain_ref, w_ref, cs_ref, wa_ref, wbd_ref, bg_ref,
                     o_ref, lg_ref, h_scr):
    j = pl.program_id(1)

    @pl.when(j == 0)
    def _():
        h = _norm_mod(x_ref[...], gain_ref[...], mod_ref, 0, 1).astype(BF16)
        h_scr[...] = h
        a = _dot(h, wa_ref[...]).astype(BF16)
        z = _dot(a, wbd_ref[...]) + bg_ref[...]
        lg_ref[...] = _log_sigmoid(z) * (1.0 / GLA_TAU)

    o_ref[...] = (_dot(h_scr[...], w_ref[...]) * cs_ref[...]).astype(BF16)


def _gla_proj(xs, mod, gain, w_bf, colscale, wa_bf, wbd_bf, bgate, rows):
    tm, tn = rows.tm, 1024
    nout = w_bf.shape[1]
    n = xs.shape[0]
    return pl.pallas_call(
        _gla_proj_kernel,
        out_shape=(jax.ShapeDtypeStruct((n, nout), BF16),
                   jax.ShapeDtypeStruct((n, 2 * GLA_K), F32)),
        grid=(rows.n, nout // tn),
        in_specs=[
            pl.BlockSpec((tm, D_MODEL), rows.blk()),
            _mod_spec(rows),
            pl.BlockSpec((1, D_MODEL), lambda t, j: (0, 0)),
            pl.BlockSpec((D_MODEL, tn), lambda t, j: (0, j)),
            pl.BlockSpec((1, tn), lambda t, j: (0, j)),
            pl.BlockSpec((D_MODEL, 128), lambda t, j: (0, 0)),
            pl.BlockSpec((128, 2 * GLA_K), lambda t, j: (0, 0)),
            pl.BlockSpec((1, 2 * GLA_K), lambda t, j: (0, 0)),
        ],
        out_specs=(pl.BlockSpec((tm, tn), lambda t, j: (t + rows.start, j)),
                   pl.BlockSpec((tm, 2 * GLA_K), lambda t, j: (t + rows.start, 0))),
        scratch_shapes=[pltpu.VMEM((tm, D_MODEL), BF16)],
        compiler_params=_cparams(("arbitrary", "arbitrary")),
        name="gla_proj",
    )(xs, mod, gain, w_bf, colscale, wa_bf, wbd_bf, bgate)


def _scan_row_maps(batch, seq):
    nlat = seq // SCAN_ROWS
    nctx_blocks = batch * CTX_LEN // SCAN_ROWS

    def fwd(b, c):
        return jnp.where(c == 0, b, nctx_blocks + b * nlat + c - 1)

    def bwd(b, c):
        return jnp.where(c == 0, b, nctx_blocks + b * nlat + nlat - c)
    return fwd, bwd, nlat + 1


def _ret_scan_kernel(ldf_ref, ldb_ref, qf_ref, kf_ref, vf_ref, qb_ref, kb_ref, vb_ref,
                     of_ref, ob_ref, sf_ref, sb_ref):
    c = pl.program_id(2)
    n = SCAN_ROWS

    @pl.when(c == 0)
    def _():
        sf_ref[...] = jnp.zeros_like(sf_ref)
        sb_ref[...] = jnp.zeros_like(sb_ref)

    ii = lax.broadcasted_iota(jnp.int32, (n, n), 0).astype(F32)
    jj = lax.broadcasted_iota(jnp.int32, (n, n), 1).astype(F32)
    pos = lax.broadcasted_iota(jnp.int32, (n, 1), 0).astype(F32)

    def one(ld_ref, q_ref, k_ref, v_ref, o_ref, s_ref, rev):
        lg = -jnp.exp(ld_ref[...])[:, :1]
        diff = (jj - ii) if rev else (ii - jj)
        dmat = jnp.where(diff >= 0, jnp.exp(lg * jnp.maximum(diff, 0.0)), 0.0)
        q_decay = jnp.exp(lg * ((n - pos) if rev else (pos + 1.0)))
        k_decay = jnp.exp(lg * (pos if rev else (n - 1.0 - pos)))
        chunk_decay = jnp.exp(lg * float(n))
        q = q_ref[...]
        k = k_ref[...]
        v = v_ref[...]
        scores = _dot_nt(q, k) * dmat
        s = s_ref[...]
        o = _dot(scores.astype(BF16), v)
        o = o + _dot((q.astype(F32) * q_decay).astype(BF16), s.astype(BF16))
        o_ref[...] = o
        s_ref[...] = s * chunk_decay + _dot_tn((k.astype(F32) * k_decay).astype(BF16), v)

    one(ldf_ref, qf_ref, kf_ref, vf_ref, of_ref, sf_ref, False)
    one(ldb_ref, qb_ref, kb_ref, vb_ref, ob_ref, sb_ref, True)


def _ret_scan(p, ld_f, ld_b, batch, seq):
    fwd, bwd, nsteps = _scan_row_maps(batch, seq)
    n = p.shape[0]
    kcol = (RET_QK + RET_V) // RET_DK
    vcol = (2 * RET_QK + RET_V) // RET_DV
    R = SCAN_ROWS

    def specs(rm):
        return [
            pl.BlockSpec((R, RET_DK), lambda b, h, c: (rm(b, c), h)),
            pl.BlockSpec((R, RET_DK), lambda b, h, c: (rm(b, c), kcol + h)),
            pl.BlockSpec((R, RET_DV), lambda b, h, c: (rm(b, c), vcol + h)),
        ]
    ld_spec = pl.BlockSpec((None, 1, 128), lambda b, h, c: (h, 0, 0))
    return pl.pallas_call(
        _ret_scan_kernel,
        out_shape=(jax.ShapeDtypeStruct((n, RET_V), F32),
                   jax.ShapeDtypeStruct((n, RET_V), F32)),
        grid=(batch, RET_HEADS, nsteps),
        in_specs=[ld_spec, ld_spec] + specs(fwd) + specs(bwd),
        out_specs=(pl.BlockSpec((R, RET_DV), lambda b, h, c: (fwd(b, c), h)),
                   pl.BlockSpec((R, RET_DV), lambda b, h, c: (bwd(b, c), h))),
        scratch_shapes=[pltpu.VMEM((RET_DK, RET_DV), F32), pltpu.VMEM((RET_DK, RET_DV), F32)],
        compiler_params=_cparams(("arbitrary", "arbitrary", "arbitrary")),
        name="ret_scan",
    )(ld_f, ld_b, p, p, p, p, p, p)


def _gla_sub_chunk(q_ref, k_ref, v_ref, g_ref, o_ref, st_ref, i, rev):
    n, r = GLA_SUB, GLA_BLK
    nb = n // r
    rows = pl.ds(pl.multiple_of(i * n, n), n)
    q = q_ref[rows, :].astype(F32)
    k = k_ref[rows, :].astype(F32)
    v = v_ref[rows, :]
    g = g_ref[rows, :]
    ri = lax.broadcasted_iota(jnp.int32, (n, n), 0)
    ci = lax.broadcasted_iota(jnp.int32, (n, n), 1)
    tri = (ci >= ri) if rev else (ci <= ri)
    tmat = jnp.where(tri, 1.0, 0.0).astype(BF16)
    g_hi = g.astype(BF16)
    g_lo = (g - g_hi.astype(F32)).astype(BF16)
    b = _dot(tmat, g_hi) + _dot(tmat, g_lo)
    b_last = b[0:1] if rev else b[n - 1:n]
    st = st_ref[...]
    o = _dot_nt((q * jnp.exp(b)).astype(BF16), st.astype(BF16))
    k_out = (k * jnp.exp(b_last - b)).astype(BF16)
    st_ref[...] = st * jnp.exp(b_last) + _dot_tn(v, k_out)
    refs = [b[bi * r:bi * r + 1] if rev else b[bi * r + r - 1:bi * r + r] for bi in range(nb)]
    k_hat = jnp.concatenate(
        [k[bi * r:(bi + 1) * r] * jnp.exp(refs[bi] - b[bi * r:(bi + 1) * r]) for bi in range(nb)],
        axis=0).astype(BF16)
    groups = []
    where = {}
    for bj in range(nb):
        for bi in (range(bj, nb) if rev else range(bj + 1)):
            where[(bj, bi)] = len(groups)
            groups.append(q[bj * r:(bj + 1) * r] * jnp.exp(b[bj * r:(bj + 1) * r] - refs[bi]))
    q_cat = jnp.concatenate(groups, axis=0).astype(BF16)
    raw = _dot_nt(q_cat, k_hat)
    col_blk = lax.broadcasted_iota(jnp.int32, (r, n), 1) // r
    row_blocks = []
    for bj in range(nb):
        acc = jnp.zeros((r, n), F32)
        for bi in (range(bj, nb) if rev else range(bj + 1)):
            gi = where[(bj, bi)]
            acc = jnp.where(col_blk == bi, raw[gi * r:(gi + 1) * r], acc)
        row_blocks.append(acc)
    scores = jnp.where(tri, jnp.concatenate(row_blocks, axis=0), 0.0)
    o_ref[rows, :] = o + _dot(scores.astype(BF16), v)


def _gla_scan_kernel(qf_ref, kf_ref, vf_ref, gf_ref, qb_ref, kb_ref, vb_ref, gb_ref,
                     of_ref, ob_ref, sf_ref, sb_ref):
    c = pl.program_id(2)
    nsub = SCAN_ROWS // GLA_SUB

    @pl.when(c == 0)
    def _():
        sf_ref[...] = jnp.zeros_like(sf_ref)
        sb_ref[...] = jnp.zeros_like(sb_ref)

    def body(i, carry):
        _gla_sub_chunk(qf_ref, kf_ref, vf_ref, gf_ref, of_ref, sf_ref, i, False)
        _gla_sub_chunk(qb_ref, kb_ref, vb_ref, gb_ref, ob_ref, sb_ref, nsub - 1 - i, True)
        return carry

    lax.fori_loop(0, nsub, body, 0)


def _gla_scan(p, lg, batch, seq):
    fwd, bwd, nsteps = _scan_row_maps(batch, seq)
    n = p.shape[0]
    kcol = (GLA_K + GLA_V) // GLA_DK
    vcol = (2 * GLA_K + GLA_V) // GLA_DV
    R = SCAN_ROWS

    def specs(rm, gcol):
        return [
            pl.BlockSpec((R, GLA_DK), lambda b, h, c: (rm(b, c), h)),
            pl.BlockSpec((R, GLA_DK), lambda b, h, c: (rm(b, c), kcol + h)),
            pl.BlockSpec((R, GLA_DV), lambda b, h, c: (rm(b, c), vcol + h)),
            pl.BlockSpec((R, GLA_DK), lambda b, h, c: (rm(b, c), gcol + h)),
        ]
    return pl.pallas_call(
        _gla_scan_kernel,
        out_shape=(jax.ShapeDtypeStruct((n, GLA_V), F32),
                   jax.ShapeDtypeStruct((n, GLA_V), F32)),
        grid=(batch, GLA_HEADS, nsteps),
        in_specs=specs(fwd, 0) + specs(bwd, GLA_HEADS),
        out_specs=(pl.BlockSpec((R, GLA_DV), lambda b, h, c: (fwd(b, c), h)),
                   pl.BlockSpec((R, GLA_DV), lambda b, h, c: (bwd(b, c), h))),
        scratch_shapes=[pltpu.VMEM((GLA_DV, GLA_DK), F32), pltpu.VMEM((GLA_DV, GLA_DK), F32)],
        compiler_params=_cparams(("arbitrary", "arbitrary", "arbitrary")),
        name="gla_scan",
    )(p, p, p, lg, p, p, p, lg)


def _ret_out_kernel(of_ref, ob_ref, g0_ref, g1_ref, x_ref, mod_ref, gnw_ref, gnb_ref, w_ref, o_ref):
    acc = jnp.zeros(o_ref.shape, F32)
    for h in range(RET_HEADS):
        sl = slice(h * RET_DV, (h + 1) * RET_DV)
        o = of_ref[:, sl] + ob_ref[:, sl]
        mu = jnp.mean(o, axis=-1, keepdims=True)
        d = o - mu
        var = jnp.mean(d * d, axis=-1, keepdims=True)
        y = d * lax.rsqrt(var + EPS) * gnw_ref[:, sl] + gnb_ref[:, sl]
        g_ref = g0_ref if h < 2 else g1_ref
        g = g_ref[:, (h % 2) * RET_DV:(h % 2 + 1) * RET_DV].astype(F32)
        acc = acc + _dot((y * _silu(g)).astype(BF16), w_ref[sl, :])
    o_ref[...] = x_ref[...] + mod_ref[2] * acc


def _ret_out(o_f, o_b, p, xs, mod, gn_w, gn_b, w_bf, rows):
    tm = rows.tm
    return pl.pallas_call(
        _ret_out_kernel,
        out_shape=jax.ShapeDtypeStruct(xs.shape, F32),
        grid=(rows.n,),
        in_specs=[
            pl.BlockSpec((tm, RET_V), rows.blk()),
            pl.BlockSpec((tm, RET_V), rows.blk()),
            pl.BlockSpec((tm, 1024), rows.blk(col=1)),
            pl.BlockSpec((tm, 1024), rows.blk(col=2)),
            pl.BlockSpec((tm, D_MODEL), rows.blk()),
            _mod_spec(rows),
            pl.BlockSpec((1, RET_V), lambda t: (0, 0)),
            pl.BlockSpec((1, RET_V), lambda t: (0, 0)),
            pl.BlockSpec((RET_V, D_MODEL), lambda t: (0, 0)),
        ],
        out_specs=pl.BlockSpec((tm, D_MODEL), rows.blk()),
        compiler_params=_cparams(("arbitrary",)),
        name="ret_out",
    )(o_f, o_b, p, p, xs, mod, gn_w, gn_b, w_bf)


def _gla_out_kernel(of_ref, ob_ref, r0_ref, r1_ref, x_ref, mod_ref, ng_ref, w_ref, o_ref):
    acc = jnp.zeros(o_ref.shape, F32)
    for h in range(GLA_HEADS):
        sl = slice(h * GLA_DV, (h + 1) * GLA_DV)
        o = of_ref[:, sl] + ob_ref[:, sl]
        y = o * lax.rsqrt(jnp.mean(o * o, axis=-1, keepdims=True) + EPS) * ng_ref[:, sl]
        r_ref = r0_ref if h < 2 else r1_ref
        r = r_ref[:, (h % 2) * GLA_DV:(h % 2 + 1) * GLA_DV].astype(F32)
        acc = acc + _dot((y * _silu(r)).astype(BF16), w_ref[sl, :])
    o_ref[...] = x_ref[...] + mod_ref[2] * acc


def _gla_out(o_f, o_b, p, xs, mod, norm_g, w_bf, rows, x_base_tile, out_rows):
    tm = rows.tm
    return pl.pallas_call(
        _gla_out_kernel,
        out_shape=jax.ShapeDtypeStruct((out_rows, D_MODEL), F32),
        grid=(rows.n,),
        in_specs=[
            pl.BlockSpec((tm, GLA_V), rows.blk()),
            pl.BlockSpec((tm, GLA_V), rows.blk()),
            pl.BlockSpec((tm, 512), rows.blk(col=1)),
            pl.BlockSpec((tm, 512), rows.blk(col=2)),
            pl.BlockSpec((tm, D_MODEL), rows.blk()),
            _mod_spec(rows),
            pl.BlockSpec((1, GLA_V), lambda t: (0, 0)),
            pl.BlockSpec((GLA_V, D_MODEL), lambda t: (0, 0)),
        ],
        out_specs=pl.BlockSpec((tm, D_MODEL), rows.blk(base_tile=x_base_tile)),
        compiler_params=_cparams(("arbitrary",)),
        name="gla_out",
    )(o_f, o_b, p, p, xs, mod, norm_g, w_bf)


def _ffn_kernel(x_ref, mod_ref, gain_ref, wg_ref, wu_ref, wd_ref, o_ref, h_scr, acc_scr):
    f = pl.program_id(1)

    @pl.when(f == 0)
    def _():
        h_scr[...] = _norm_mod(x_ref[...], gain_ref[...], mod_ref, 3, 4).astype(BF16)
        acc_scr[...] = jnp.zeros_like(acc_scr)

    h = h_scr[...]
    hid = (_silu(_dot(h, wg_ref[...])) * _dot(h, wu_ref[...])).astype(BF16)
    acc_scr[...] += _dot(hid, wd_ref[...])

    @pl.when(f == pl.num_programs(1) - 1)
    def _():
        o_ref[...] = x_ref[...] + mod_ref[5] * acc_scr[...]


def _ffn(xs, mod, gain, wg_bf, wu_bf, wd_bf, rows):
    tm, tf = rows.tm, D_FF // 2
    return pl.pallas_call(
        _ffn_kernel,
        out_shape=jax.ShapeDtypeStruct(xs.shape, F32),
        grid=(rows.n, D_FF // tf),
        in_specs=[
            pl.BlockSpec((tm, D_MODEL), rows.blk()),
            _mod_spec(rows),
            pl.BlockSpec((1, D_MODEL), lambda t, f: (0, 0)),
            pl.BlockSpec((D_MODEL, tf), lambda t, f: (0, f)),
            pl.BlockSpec((D_MODEL, tf), lambda t, f: (0, f)),
            pl.BlockSpec((tf, D_MODEL), lambda t, f: (f, 0)),
        ],
        out_specs=pl.BlockSpec((tm, D_MODEL), rows.blk()),
        scratch_shapes=[pltpu.VMEM((tm, D_MODEL), BF16), pltpu.VMEM((tm, D_MODEL), F32)],
        compiler_params=_cparams(("arbitrary", "arbitrary")),
        name="ffn",
    )(xs, mod, gain, wg_bf, wu_bf, wd_bf)


def _router_kernel(x_ref, mod_ref, gain_ref, wr_hi_ref, wr_lo_ref, h_ref, idx_ref, wt_ref):
    h = _norm_mod(x_ref[...], gain_ref[...], mod_ref, 3, 4)
    h_ref[...] = h
    h_hi = h.astype(BF16)
    h_lo = (h - h_hi.astype(F32)).astype(BF16)
    logits = _dot(h_hi, wr_hi_ref[...]) + _dot(h_lo, wr_hi_ref[...]) + _dot(h_hi, wr_lo_ref[...])
    lane = lax.broadcasted_iota(jnp.int32, logits.shape, 1)
    lane_f = lane.astype(F32)
    neg = jnp.float32(-jnp.inf)
    l1 = jnp.where(lane < N_EXPERTS, logits, neg)
    m1 = jnp.max(l1, axis=-1, keepdims=True)
    i1 = jnp.min(jnp.where(l1 == m1, lane_f, 128.0), axis=-1, keepdims=True)
    l2 = jnp.where(lane_f == i1, neg, l1)
    m2 = jnp.max(l2, axis=-1, keepdims=True)
    i2 = jnp.min(jnp.where(l2 == m2, lane_f, 128.0), axis=-1, keepdims=True)
    e2 = jnp.exp(m2 - m1)
    den = 1.0 + e2
    idx_ref[...] = jnp.where(lane == 0, i1, jnp.where(lane == 1, i2, 0.0)).astype(jnp.int32)
    wt_ref[...] = jnp.where(lane == 0, 1.0 / den, jnp.where(lane == 1, e2 / den, 0.0))


def _router(xs, mod, gain, wr_hi, wr_lo, rows, x_base_tile):
    tm = rows.tm
    nrow = rows.n * tm
    out_blk = lambda t: (t, 0)
    return pl.pallas_call(
        _router_kernel,
        out_shape=(jax.ShapeDtypeStruct((nrow, D_MODEL), F32),
                   jax.ShapeDtypeStruct((nrow, 128), jnp.int32),
                   jax.ShapeDtypeStruct((nrow, 128), F32)),
        grid=(rows.n,),
        in_specs=[
            pl.BlockSpec((tm, D_MODEL), rows.blk(base_tile=x_base_tile)),
            _mod_spec(rows),
            pl.BlockSpec((1, D_MODEL), lambda t: (0, 0)),
            pl.BlockSpec((D_MODEL, 128), lambda t: (0, 0)),
            pl.BlockSpec((D_MODEL, 128), lambda t: (0, 0)),
        ],
        out_specs=(pl.BlockSpec((tm, D_MODEL), out_blk),
                   pl.BlockSpec((tm, 128), out_blk),
                   pl.BlockSpec((tm, 128), out_blk)),
        compiler_params=_cparams(("arbitrary",)),
        name="moe_router",
    )(xs, mod, gain, wr_hi, wr_lo)


GATHER_ROWS = 512


def _gather_kernel(tok_ref, src_ref, dst_ref, sem):
    base = pl.program_id(0) * GATHER_ROWS

    def copy(r):
        tok = tok_ref[base + r]
        return pltpu.make_async_copy(src_ref.at[pl.ds(tok, 1)], dst_ref.at[pl.ds(base + r, 1)], sem)

    def start(r, carry):
        copy(r).start()
        return carry

    def wait(r, carry):
        copy(r).wait()
        return carry

    lax.fori_loop(0, GATHER_ROWS, start, 0)
    lax.fori_loop(0, GATHER_ROWS, wait, 0)


def _gather_rows(tok_of_slot, src):
    nslot = tok_of_slot.shape[0]
    return pl.pallas_call(
        _gather_kernel,
        out_shape=jax.ShapeDtypeStruct((nslot, D_MODEL), F32),
        grid_spec=pltpu.PrefetchScalarGridSpec(
            num_scalar_prefetch=1,
            grid=(nslot // GATHER_ROWS,),
            in_specs=[pl.BlockSpec(memory_space=pl.ANY)],
            out_specs=pl.BlockSpec(memory_space=pl.ANY),
            scratch_shapes=[pltpu.SemaphoreType.DMA],
        ),
        compiler_params=_cparams(("arbitrary",)),
        name="moe_gather",
    )(tok_of_slot, src)


MOE_TM = 512
MOE_TF = 512


def _expert_kernel(te_ref, ta_ref, hs_ref, wg_ref, wu_ref, wd_ref, o_ref, hb_scr, acc_scr):
    t = pl.program_id(0)
    f = pl.program_id(1)
    active = ta_ref[t] == 1

    @pl.when(f == 0)
    def _():
        hb_scr[...] = hs_ref[...].astype(BF16)
        acc_scr[...] = jnp.zeros_like(acc_scr)

    @pl.when(active)
    def _():
        h = hb_scr[...]
        hid = (_silu(_dot(h, wg_ref[...])) * _dot(h, wu_ref[...])).astype(BF16)
        acc_scr[...] += _dot(hid, wd_ref[...])

    @pl.when(f == pl.num_programs(1) - 1)
    def _():
        o_ref[...] = acc_scr[...]


def _experts(tile_expert, tile_active, hs, wg_bf, wu_bf, wd_bf):
    nslot = hs.shape[0]
    nf = D_FF_EXPERT // MOE_TF

    def fcol(t, f, te, ta):
        return jnp.where(ta[t] == 1, f, nf - 1)
    return pl.pallas_call(
        _expert_kernel,
        out_shape=jax.ShapeDtypeStruct((nslot, D_MODEL), F32),
        grid_spec=pltpu.PrefetchScalarGridSpec(
            num_scalar_prefetch=2,
            grid=(nslot // MOE_TM, nf),
            in_specs=[
                pl.BlockSpec((MOE_TM, D_MODEL), lambda t, f, te, ta: (t, 0)),
                pl.BlockSpec((None, D_MODEL, MOE_TF), lambda t, f, te, ta: (te[t], 0, fcol(t, f, te, ta))),
                pl.BlockSpec((None, D_MODEL, MOE_TF), lambda t, f, te, ta: (te[t], 0, fcol(t, f, te, ta))),
                pl.BlockSpec((None, MOE_TF, D_MODEL), lambda t, f, te, ta: (te[t], fcol(t, f, te, ta), 0)),
            ],
            out_specs=pl.BlockSpec((MOE_TM, D_MODEL), lambda t, f, te, ta: (t, 0)),
            scratch_shapes=[pltpu.VMEM((MOE_TM, D_MODEL), BF16), pltpu.VMEM((MOE_TM, D_MODEL), F32)],
        ),
        compiler_params=_cparams(("arbitrary", "arbitrary")),
        name="moe_experts",
    )(tile_expert, tile_active, hs, wg_bf, wu_bf, wd_bf)


COMBINE_TM = 256


def _combine_kernel(pos_ref, ys_ref, x_ref, wt_ref, mod_ref, fg_ref, o_ref, buf, sem, *, final):
    base = pl.program_id(0) * COMBINE_TM

    def copy(r, k):
        p = pos_ref[2 * (base + r) + k]
        return pltpu.make_async_copy(ys_ref.at[pl.ds(p, 1)], buf.at[k, pl.ds(r, 1)], sem)

    def start(r, carry):
        copy(r, 0).start()
        copy(r, 1).start()
        return carry

    def wait(r, carry):
        copy(r, 0).wait()
        copy(r, 1).wait()
        return carry

    lax.fori_loop(0, COMBINE_TM, start, 0)
    lax.fori_loop(0, COMBINE_TM, wait, 0)
    w = wt_ref[...]
    f = w[:, 0:1] * buf[0] + w[:, 1:2] * buf[1]
    y = x_ref[...] + mod_ref[5] * f
    if final:
        ms = jnp.mean(y * y, axis=-1, keepdims=True)
        y = y * lax.rsqrt(ms + EPS) * fg_ref[...]
    o_ref[...] = y


def _combine(pos, ys, xs, wts, mod, final_g, rows, x_base_tile, final):
    tm = rows.tm
    nrow = rows.n * tm
    return pl.pallas_call(
        functools.partial(_combine_kernel, final=final),
        out_shape=jax.ShapeDtypeStruct((nrow, D_MODEL), F32),
        grid_spec=pltpu.PrefetchScalarGridSpec(
            num_scalar_prefetch=1,
            grid=(rows.n,),
            in_specs=[
                pl.BlockSpec(memory_space=pl.ANY),
                pl.BlockSpec((tm, D_MODEL), rows.blk(base_tile=x_base_tile)),
                pl.BlockSpec((tm, 128), lambda t, p: (t, 0)),
                _mod_spec(rows),
                pl.BlockSpec((1, D_MODEL), lambda t, p: (0, 0)),
            ],
            out_specs=pl.BlockSpec((tm, D_MODEL), lambda t, p: (t, 0)),
            scratch_shapes=[pltpu.VMEM((2, tm, D_MODEL), F32), pltpu.SemaphoreType.DMA],
        ),
        compiler_params=_cparams(("arbitrary",)),
        name="moe_combine",
    )(pos, ys, xs, wts, mod, final_g)


def _routing_tables(idx, nslot_pad):
    nrow = idx.shape[0]
    e_flat = idx[:, :2].reshape(-1)
    onehot = (e_flat[:, None] == jnp.arange(N_EXPERTS, dtype=jnp.int32)[None, :]).astype(jnp.int32)
    csum = jnp.cumsum(onehot, axis=0)
    rank = jnp.sum(csum * onehot, axis=1) - 1
    counts = csum[-1]
    padded = ((counts + MOE_TM - 1) // MOE_TM) * MOE_TM
    ends = jnp.cumsum(padded)
    offs = ends - padded
    pos = (jnp.sum(onehot * offs[None, :], axis=1) + rank).astype(jnp.int32)
    tok = jnp.arange(2 * nrow, dtype=jnp.int32) // 2
    tok_of_slot = jnp.zeros((nslot_pad,), jnp.int32).at[pos].set(tok)
    ntile = nslot_pad // MOE_TM
    tile_start = jnp.arange(ntile, dtype=jnp.int32) * MOE_TM
    tile_active = (tile_start < ends[-1]).astype(jnp.int32)
    last_start = jnp.maximum(ends[-1] - MOE_TM, 0)
    tile_expert = jnp.sum((jnp.minimum(tile_start, last_start)[:, None] >= ends[None, :]).astype(jnp.int32),
                          axis=1)
    tile_expert = jnp.minimum(tile_expert, N_EXPERTS - 1).astype(jnp.int32)
    return pos, tok_of_slot, tile_expert, tile_active


def _moe(xs, mod, gain, wr, wg_bf, wu_bf, wd_bf, final_g, rows, x_base_tile, final):
    wr_pad = jnp.zeros((D_MODEL, 128), F32).at[:, :N_EXPERTS].set(wr)
    wr_hi = wr_pad.astype(BF16)
    wr_lo = (wr_pad - wr_hi.astype(F32)).astype(BF16)
    h, idx, wts = _router(xs, mod, gain, wr_hi, wr_lo, rows, x_base_tile)
    nrow = h.shape[0]
    nslot_pad = 2 * nrow + N_EXPERTS * MOE_TM
    pos, tok_of_slot, tile_expert, tile_active = _routing_tables(idx, nslot_pad)
    hs = _gather_rows(tok_of_slot, h)
    ys = _experts(tile_expert, tile_active, hs, wg_bf, wu_bf, wd_bf)
    return _combine(pos, ys, xs, wts, mod, final_g, rows, x_base_tile, final)


def kernel(x, c, ctx, c_ctx, ada_w, ada_b, norm_mix_g, norm_ffn_g, final_g, ret_w_in, ret_log_decay,
           ret_gn_w, ret_gn_b, ret_w_out, gla_w_in, gla_w_gate_up, gla_b_gate, gla_norm_g, gla_w_out,
           ffn_w_gate, ffn_w_up, ffn_w_down, moe_w_router, moe_w_gate, moe_w_up, moe_w_down):
    batch, seq, d = x.shape
    assert d == D_MODEL and ctx.shape == (batch, CTX_LEN, d) and seq % 1024 == 0
    nctx = batch * CTX_LEN
    xs = jnp.concatenate([ctx.reshape(nctx, d), x.reshape(batch * seq, d)], axis=0)
    mod_all = _modulation(c, c_ctx, ada_w, ada_b)

    proj_tm = 1024
    cos, sin = _rope_tables(seq, proj_tm)
    ret_colscale = jnp.concatenate([
        jnp.ones((1, RET_QK + RET_V), F32),
        jnp.full((1, RET_QK), RET_DK ** -0.5, F32),
        jnp.ones((1, RET_V), F32)], axis=1)
    gla_main = 2 * GLA_K + 2 * GLA_V
    gla_colscale = jnp.concatenate([
        jnp.full((1, GLA_K), GLA_DK ** -0.5, F32),
        jnp.ones((1, gla_main - GLA_K), F32)], axis=1)

    final_row = final_g.reshape(1, d)
    for i in range(DEPTH):
        last = i == DEPTH - 1
        j = i // 2
        mod = mod_all[i]
        rows_all_proj = _Rows(batch, seq, proj_tm, False)
        rows_all = _Rows(batch, seq, 512, False)
        rows_lat = _Rows(batch, seq, 512, True)
        mix_gain = norm_mix_g[i].reshape(1, d)
        ffn_gain = norm_ffn_g[i].reshape(1, d)
        if i % 2 == 0:
            p = _ret_proj(xs, mod, mix_gain, ret_w_in[j].astype(BF16), ret_colscale, cos, sin,
                          rows_all_proj)
            ld = jnp.broadcast_to(ret_log_decay[j][:, :, None, None], (2, RET_HEADS, 1, 128))
            o_f, o_b = _ret_scan(p, ld[0], ld[1], batch, seq)
            xs = _ret_out(o_f, o_b, p, xs, mod, ret_gn_w[j].reshape(1, RET_V),
                          ret_gn_b[j].reshape(1, RET_V), ret_w_out[j].astype(BF16), rows_all)
            xs = _ffn(xs, mod, ffn_gain, ffn_w_gate[j].astype(BF16), ffn_w_up[j].astype(BF16),
                      ffn_w_down[j].astype(BF16), rows_all)
        else:
            w_in = gla_w_in[j]
            wa = jnp.zeros((d, 128), F32).at[:, :2 * GLA_GATE_RANK].set(w_in[:, gla_main:])
            wbd = jnp.zeros((128, 2 * GLA_K), F32)
            wbd = wbd.at[:GLA_GATE_RANK, :GLA_K].set(gla_w_gate_up[j, 0])
            wbd = wbd.at[GLA_GATE_RANK:2 * GLA_GATE_RANK, GLA_K:].set(gla_w_gate_up[j, 1])
            p, lg = _gla_proj(xs, mod, mix_gain, w_in[:, :gla_main].astype(BF16), gla_colscale,
                              wa.astype(BF16), wbd.astype(BF16), gla_b_gate[j].reshape(1, 2 * GLA_K),
                              rows_all_proj)
            o_f, o_b = _gla_scan(p, lg, batch, seq)
            rows = rows_lat if last else rows_all
            base = rows.start if last else 0
            xs = _gla_out(o_f, o_b, p, xs, mod, gla_norm_g[j].reshape(1, GLA_V),
                          gla_w_out[j].astype(BF16), rows, base, rows.n * rows.tm)
            rows_c = _Rows(batch, seq, COMBINE_TM, last)
            base_c = rows_c.start if last else 0
            xs = _moe(xs, mod, ffn_gain, moe_w_router[j], moe_w_gate[j].astype(BF16),
                      moe_w_up[j].astype(BF16), moe_w_down[j].astype(BF16), final_row,
                      rows_c, base_c, last)
    return xs.reshape(batch, seq, d)
```

```python
import functools

import jax
import jax.numpy as jnp
from jax import lax
from jax.experimental import pallas as pl
from jax.experimental.pallas import tpu as pltpu

D_MODEL = 1024
GRID_W = 64
CTX_LEN = 256
DEPTH = 4
RET_HEADS = 4
RET_DK = 256
RET_DV = 512
RET_QK = 1024
RET_V = 2048
GLA_HEADS = 4
GLA_DK = 128
GLA_DV = 256
GLA_K = 512
GLA_V = 1024
GLA_GATE_RANK = 16
GLA_TAU = 16.0
D_FF = 2816
N_EXPERTS = 8
D_FF_EXPERT = 3584
ROPE_BASE = 10000.0
EPS = 1e-6

BF16 = jnp.bfloat16
F32 = jnp.float32

SCAN_ROWS = 256
GLA_SUB = 128
GLA_BLK = 16
VMEM_LIMIT = 48 * 1024 * 1024


def _cparams(sem):
    return pltpu.CompilerParams(dimension_semantics=sem, vmem_limit_bytes=VMEM_LIMIT)


def _dot(a, b):
    return jnp.dot(a, b, preferred_element_type=F32)


def _dot_nt(a, b):
    return lax.dot_general(a, b, (((1,), (1,)), ((), ())), preferred_element_type=F32)


def _dot_tn(a, b):
    return lax.dot_general(a, b, (((0,), (0,)), ((), ())), preferred_element_type=F32)


def _silu(x):
    return x * jax.nn.sigmoid(x)


def _norm_mod(x, gain, mod_ref, shift_i, scale_i):
    ms = jnp.mean(x * x, axis=-1, keepdims=True)
    y = x * lax.rsqrt(ms + EPS) * gain
    return y * (1.0 + mod_ref[scale_i]) + mod_ref[shift_i]


class _Rows:
    def __init__(self, batch, seq, tm, lat_only):
        self.tm = tm
        self.nct = batch * CTX_LEN // tm
        self.tpb = seq // tm
        self.start = self.nct if lat_only else 0
        self.n = batch * seq // tm + (0 if lat_only else self.nct)
        self.batch = batch

    def blk(self, base_tile=0, col=0):
        off = self.start - base_tile
        return lambda t, *_: (t + off, col)

    def mod_blk(self):
        start, nct, tpb, batch = self.start, self.nct, self.tpb, self.batch

        def im(t, *_):
            tg = t + start
            return (jnp.where(tg < nct, batch, (tg - nct) // tpb), 0, 0, 0)
        return im

    def pos_blk(self):
        start, nct, tpb = self.start, self.nct, self.tpb

        def im(t, *_):
            tg = t + start
            return (jnp.where(tg < nct, 0, 1 + (tg - nct) % tpb), 0)
        return im


def _mod_spec(rows):
    return pl.BlockSpec((None, 6, 1, D_MODEL), rows.mod_blk())


def _mod_kernel(c_ref, w_ref, b_ref, o_ref):
    s = _silu(c_ref[...]).astype(BF16)
    o_ref[...] = _dot(s, w_ref[...].astype(BF16)) + b_ref[...]


def _modulation(c, c_ctx, ada_w, ada_b):
    batch = c.shape[0]
    nrow = 16
    cc = jnp.concatenate([c, c_ctx[None], jnp.zeros((nrow - batch - 1, D_MODEL), F32)], axis=0)
    out = pl.pallas_call(
        _mod_kernel,
        out_shape=jax.ShapeDtypeStruct((DEPTH, 6, nrow, D_MODEL), F32),
        grid=(DEPTH, 6),
        in_specs=[
            pl.BlockSpec((nrow, D_MODEL), lambda i, k: (0, 0)),
            pl.BlockSpec((None, D_MODEL, D_MODEL), lambda i, k: (i, 0, k)),
            pl.BlockSpec((None, None, 1, D_MODEL), lambda i, k: (i, k, 0, 0)),
        ],
        out_specs=pl.BlockSpec((None, None, nrow, D_MODEL), lambda i, k: (i, k, 0, 0)),
        compiler_params=_cparams(("arbitrary", "arbitrary")),
        name="adaln_mod",
    )(cc, ada_w, ada_b.reshape(DEPTH, 6, 1, D_MODEL))
    return out.transpose(0, 2, 1, 3)[:, :, :, None, :]


def _ret_proj_kernel(x_ref, mod_ref, gain_ref, w_ref, cs_ref, cos_ref, sin_ref, o_ref, h_scr):
    j = pl.program_id(1)

    @pl.when(j == 0)
    def _():
        h_scr[...] = _norm_mod(x_ref[...], gain_ref[...], mod_ref, 0, 1).astype(BF16)

    acc = _dot(h_scr[...], w_ref[...]) * cs_ref[...]
    is_rope = jnp.logical_or(j == 0, j == 3)

    @pl.when(is_rope)
    def _():
        for blk in range(D_MODEL // 128):
            lo = blk * 128
            tl = (blk % 2) * 128
            a = acc[:, lo:lo + 128]
            rot = pltpu.roll(a, 64, 1)
            o_ref[:, lo:lo + 128] = (a * cos_ref[:, tl:tl + 128]
                                     + rot * sin_ref[:, tl:tl + 128]).astype(BF16)

    @pl.when(jnp.logical_not(is_rope))
    def _():
        o_ref[...] = acc.astype(BF16)


def _rope_tables(seq, tm):
    half = RET_DK // 4
    freqs = ROPE_BASE ** (-jnp.arange(half, dtype=F32) / half)
    rows = seq // GRID_W
    row = jnp.broadcast_to(jnp.arange(rows, dtype=F32)[:, None], (rows, GRID_W)).reshape(seq)
    col = jnp.broadcast_to(jnp.arange(GRID_W, dtype=F32)[None, :], (rows, GRID_W)).reshape(seq)
    ar = row[:, None] * freqs[None, :]
    ac = col[:, None] * freqs[None, :]
    cos = jnp.concatenate([jnp.cos(ar), jnp.cos(ar), jnp.cos(ac), jnp.cos(ac)], axis=-1)
    sin = jnp.concatenate([-jnp.sin(ar), jnp.sin(ar), -jnp.sin(ac), jnp.sin(ac)], axis=-1)
    cos = jnp.concatenate([jnp.ones((tm, RET_DK), F32), cos], axis=0)
    sin = jnp.concatenate([jnp.zeros((tm, RET_DK), F32), sin], axis=0)
    return cos, sin


def _ret_proj(xs, mod, gain, w_bf, colscale, cos, sin, rows):
    tm, tn = rows.tm, 1024
    nout = w_bf.shape[1]
    return pl.pallas_call(
        _ret_proj_kernel,
        out_shape=jax.ShapeDtypeStruct((xs.shape[0], nout), BF16),
        grid=(rows.n, nout // tn),
        in_specs=[
            pl.BlockSpec((tm, D_MODEL), rows.blk()),
            _mod_spec(rows),
            pl.BlockSpec((1, D_MODEL), lambda t, j: (0, 0)),
            pl.BlockSpec((D_MODEL, tn), lambda t, j: (0, j)),
            pl.BlockSpec((1, tn), lambda t, j: (0, j)),
            pl.BlockSpec((tm, RET_DK), rows.pos_blk()),
            pl.BlockSpec((tm, RET_DK), rows.pos_blk()),
        ],
        out_specs=pl.BlockSpec((tm, tn), lambda t, j: (t + rows.start, j)),
        scratch_shapes=[pltpu.VMEM((tm, D_MODEL), BF16)],
        compiler_params=_cparams(("arbitrary", "arbitrary")),
        name="ret_proj",
    )(xs, mod, gain, w_bf, colscale, cos, sin)


def _log_sigmoid(z):
    return jnp.minimum(z, 0.0) - jnp.log1p(jnp.exp(-jnp.abs(z)))


def _gla_proj_kernel(x_ref, mod_ref, gain_ref, w_ref, cs_ref, wa_ref, wbd_ref, bg_ref,
                     o_ref, lg_ref, h_scr):
    j = pl.program_id(1)

    @pl.when(j == 0)
    def _():
        h = _norm_mod(x_ref[...], gain_ref[...], mod_ref, 0, 1).astype(BF16)
        h_scr[...] = h
        a = _dot(h, wa_ref[...]).astype(BF16)
        z = _dot(a, wbd_ref[...]) + bg_ref[...]
        lg_ref[...] = _log_sigmoid(z) * (1.0 / GLA_TAU)

    o_ref[...] = (_dot(h_scr[...], w_ref[...]) * cs_ref[...]).astype(BF16)


def _gla_proj(xs, mod, gain, w_bf, colscale, wa_bf, wbd_bf, bgate, rows):
    tm, tn = rows.tm, 1024
    nout = w_bf.shape[1]
    n = xs.shape[0]
    return pl.pallas_call(
        _gla_proj_kernel,
        out_shape=(jax.ShapeDtypeStruct((n, nout), BF16),
                   jax.ShapeDtypeStruct((n, 2 * GLA_K), F32)),
        grid=(rows.n, nout // tn),
        in_specs=[
            pl.BlockSpec((tm, D_MODEL), rows.blk()),
            _mod_spec(rows),
            pl.BlockSpec((1, D_MODEL), lambda t, j: (0, 0)),
            pl.BlockSpec((D_MODEL, tn), lambda t, j: (0, j)),
            pl.BlockSpec((1, tn), lambda t, j: (0, j)),
            pl.BlockSpec((D_MODEL, 128), lambda t, j: (0, 0)),
            pl.BlockSpec((128, 2 * GLA_K), lambda t, j: (0, 0)),
            pl.BlockSpec((1, 2 * GLA_K), lambda t, j: (0, 0)),
        ],
        out_specs=(pl.BlockSpec((tm, tn), lambda t, j: (t + rows.start, j)),
                   pl.BlockSpec((tm, 2 * GLA_K), lambda t, j: (t + rows.start, 0))),
        scratch_shapes=[pltpu.VMEM((tm, D_MODEL), BF16)],
        compiler_params=_cparams(("arbitrary", "arbitrary")),
        name="gla_proj",
    )(xs, mod, gain, w_bf, colscale, wa_bf, wbd_bf, bgate)


def _scan_row_maps(batch, seq):
    nlat = seq // SCAN_ROWS
    nctx_blocks = batch * CTX_LEN // SCAN_ROWS

    def fwd(b, c):
        return jnp.where(c == 0, b, nctx_blocks + b * nlat + c - 1)

    def bwd(b, c):
        return jnp.where(c == 0, b, nctx_blocks + b * nlat + nlat - c)
    return fwd, bwd, nlat + 1


def _ret_scan_kernel(ld_ref, qf_ref, kf_ref, vf_ref, qb_ref, kb_ref, vb_ref,
                     of_ref, ob_ref, sf_ref, sb_ref, dm_ref, qd_ref, kd_ref, cd_ref):
    b = pl.program_id(0)
    c = pl.program_id(1)
    n = SCAN_ROWS

    @pl.when(jnp.logical_and(b == 0, c == 0))
    def _():
        ii = lax.broadcasted_iota(jnp.int32, (n, n), 0).astype(F32)
        jj = lax.broadcasted_iota(jnp.int32, (n, n), 1).astype(F32)
        for d in range(2):
            rev = d == 1
            for h in range(RET_HEADS):
                lg = -jnp.exp(ld_ref[d * RET_HEADS + h])[:, :1]
                diff = (jj - ii) if rev else (ii - jj)
                dm_ref[d, h] = jnp.where(diff >= 0, jnp.exp(lg * jnp.maximum(diff, 0.0)), 0.0)
                qd_ref[d, h] = jnp.exp(lg * ((n - ii) if rev else (ii + 1.0)))
                kd_ref[d, h] = jnp.exp(lg * (ii if rev else (n - 1.0 - ii)))
                cd_ref[d, h] = jnp.broadcast_to(jnp.exp(lg * float(n)), (8, 128))

    @pl.when(c == 0)
    def _():
        sf_ref[...] = jnp.zeros_like(sf_ref)
        sb_ref[...] = jnp.zeros_like(sb_ref)

    def one(d, h, q_ref, k_ref, v_ref, o_ref, s_ref):
        qk = slice(h * RET_DK, (h + 1) * RET_DK)
        vv = slice(h * RET_DV, (h + 1) * RET_DV)
        q = q_ref[:, qk]
        k = k_ref[:, qk]
        v = v_ref[:, vv]
        scores = _dot_nt(q, k) * dm_ref[d, h]
        s = s_ref[h]
        o = _dot(scores.astype(BF16), v)
        o = o + _dot((q.astype(F32) * qd_ref[d, h]).astype(BF16), s.astype(BF16))
        o_ref[:, vv] = o
        s_ref[h] = s * cd_ref[d, h][:1, :1] + _dot_tn((k.astype(F32) * kd_ref[d, h]).astype(BF16), v)

    for h in range(RET_HEADS):
        one(0, h, qf_ref, kf_ref, vf_ref, of_ref, sf_ref)
        one(1, h, qb_ref, kb_ref, vb_ref, ob_ref, sb_ref)


def _ret_scan(p, ld, batch, seq):
    fwd, bwd, nsteps = _scan_row_maps(batch, seq)
    n = p.shape[0]
    kcol = (RET_QK + RET_V) // RET_QK
    vcol = (2 * RET_QK + RET_V) // RET_V
    R = SCAN_ROWS

    def specs(rm):
        return [
            pl.BlockSpec((R, RET_QK), lambda b, c: (rm(b, c), 0)),
            pl.BlockSpec((R, RET_QK), lambda b, c: (rm(b, c), kcol)),
            pl.BlockSpec((R, RET_V), lambda b, c: (rm(b, c), vcol)),
        ]
    state = pltpu.VMEM((RET_HEADS, RET_DK, RET_DV), F32)
    table = pltpu.VMEM((2, RET_HEADS, R, R), F32)
    return pl.pallas_call(
        _ret_scan_kernel,
        out_shape=(jax.ShapeDtypeStruct((n, RET_V), F32),
                   jax.ShapeDtypeStruct((n, RET_V), F32)),
        grid=(batch, nsteps),
        in_specs=[pl.BlockSpec((2 * RET_HEADS, 1, 128), lambda b, c: (0, 0, 0))]
        + specs(fwd) + specs(bwd),
        out_specs=(pl.BlockSpec((R, RET_V), lambda b, c: (fwd(b, c), 0)),
                   pl.BlockSpec((R, RET_V), lambda b, c: (bwd(b, c), 0))),
        scratch_shapes=[state, state, table, table, table,
                        pltpu.VMEM((2, RET_HEADS, 8, 128), F32)],
        compiler_params=_cparams(("arbitrary", "arbitrary")),
        name="ret_scan",
    )(ld, p, p, p, p, p, p)


def _gla_tri(rev):
    n = GLA_SUB
    ri = lax.broadcasted_iota(jnp.int32, (n, n), 0)
    ci = lax.broadcasted_iota(jnp.int32, (n, n), 1)
    return (ci >= ri) if rev else (ci <= ri)


def _gla_cum_gates(g_ref, i, rev):
    n = GLA_SUB
    g = g_ref[pl.ds(pl.multiple_of(i * n, n), n), :]
    tmat = jnp.where(_gla_tri(rev), 1.0, 0.0).astype(BF16)
    g_hi = g.astype(BF16)
    g_lo = (g - g_hi.astype(F32)).astype(BF16)
    return _dot(tmat, g_hi) + _dot(tmat, g_lo)


def _gla_sub_chunk(q_ref, k_ref, v_ref, b_all, o_ref, st_ref, h, i, rev):
    n, r = GLA_SUB, GLA_BLK
    nb = n // r
    rows = pl.ds(pl.multiple_of(i * n, n), n)
    kcols = slice(h * GLA_DK, (h + 1) * GLA_DK)
    vcols = slice(h * GLA_DV, (h + 1) * GLA_DV)
    q = q_ref[rows, kcols].astype(F32)
    k = k_ref[rows, kcols].astype(F32)
    v = v_ref[rows, vcols]
    tri = _gla_tri(rev)
    b = b_all[:, kcols]
    b_last = b[0:1] if rev else b[n - 1:n]
    st = st_ref[h]
    o = _dot_nt((q * jnp.exp(b)).astype(BF16), st.astype(BF16))
    k_out = (k * jnp.exp(b_last - b)).astype(BF16)
    st_ref[h] = st * jnp.exp(b_last) + _dot_tn(v, k_out)
    refs = [b[bi * r:bi * r + 1] if rev else b[bi * r + r - 1:bi * r + r] for bi in range(nb)]
    k_hat = jnp.concatenate(
        [k[bi * r:(bi + 1) * r] * jnp.exp(refs[bi] - b[bi * r:(bi + 1) * r]) for bi in range(nb)],
        axis=0).astype(BF16)
    groups = []
    where = {}
    for bj in range(nb):
        for bi in (range(bj, nb) if rev else range(bj + 1)):
            where[(bj, bi)] = len(groups)
            groups.append(q[bj * r:(bj + 1) * r] * jnp.exp(b[bj * r:(bj + 1) * r] - refs[bi]))
    q_cat = jnp.concatenate(groups, axis=0).astype(BF16)
    raw = _dot_nt(q_cat, k_hat)
    col_blk = lax.broadcasted_iota(jnp.int32, (r, n), 1) // r
    row_blocks = []
    for bj in range(nb):
        acc = jnp.zeros((r, n), F32)
        for bi in (range(bj, nb) if rev else range(bj + 1)):
            gi = where[(bj, bi)]
            acc = jnp.where(col_blk == bi, raw[gi * r:(gi + 1) * r], acc)
        row_blocks.append(acc)
    scores = jnp.where(tri, jnp.concatenate(row_blocks, axis=0), 0.0)
    o_ref[rows, vcols] = o + _dot(scores.astype(BF16), v)


def _gla_scan_kernel(qf_ref, kf_ref, vf_ref, gf_ref, qb_ref, kb_ref, vb_ref, gb_ref,
                     of_ref, ob_ref, sf_ref, sb_ref):
    c = pl.program_id(1)
    nsub = SCAN_ROWS // GLA_SUB

    @pl.when(c == 0)
    def _():
        sf_ref[...] = jnp.zeros_like(sf_ref)
        sb_ref[...] = jnp.zeros_like(sb_ref)

    def body(i, carry):
        ib = nsub - 1 - i
        b_f = _gla_cum_gates(gf_ref, i, False)
        b_b = _gla_cum_gates(gb_ref, ib, True)
        for h in range(GLA_HEADS):
            _gla_sub_chunk(qf_ref, kf_ref, vf_ref, b_f, of_ref, sf_ref, h, i, False)
            _gla_sub_chunk(qb_ref, kb_ref, vb_ref, b_b, ob_ref, sb_ref, h, ib, True)
        return carry

    lax.fori_loop(0, nsub, body, 0)


def _gla_scan(p, lg, batch, seq):
    fwd, bwd, nsteps = _scan_row_maps(batch, seq)
    n = p.shape[0]
    kcol = (GLA_K + GLA_V) // GLA_K
    vcol = (2 * GLA_K + GLA_V) // GLA_V
    R = SCAN_ROWS

    def specs(rm, gcol):
        return [
            pl.BlockSpec((R, GLA_K), lambda b, c: (rm(b, c), 0)),
            pl.BlockSpec((R, GLA_K), lambda b, c: (rm(b, c), kcol)),
            pl.BlockSpec((R, GLA_V), lambda b, c: (rm(b, c), vcol)),
            pl.BlockSpec((R, GLA_K), lambda b, c: (rm(b, c), gcol)),
        ]
    state = pltpu.VMEM((GLA_HEADS, GLA_DV, GLA_DK), F32)
    return pl.pallas_call(
        _gla_scan_kernel,
        out_shape=(jax.ShapeDtypeStruct((n, GLA_V), F32),
                   jax.ShapeDtypeStruct((n, GLA_V), F32)),
        grid=(batch, nsteps),
        in_specs=specs(fwd, 0) + specs(bwd, 1),
        out_specs=(pl.BlockSpec((R, GLA_V), lambda b, c: (fwd(b, c), 0)),
                   pl.BlockSpec((R, GLA_V), lambda b, c: (bwd(b, c), 0))),
        scratch_shapes=[state, state],
        compiler_params=_cparams(("arbitrary", "arbitrary")),
        name="gla_scan",
    )(p, p, p, lg, p, p, p, lg)


def _ret_out_kernel(of_ref, ob_ref, g0_ref, g1_ref, x_ref, mod_ref, gnw_ref, gnb_ref, w_ref, o_ref):
    acc = jnp.zeros(o_ref.shape, F32)
    for h in range(RET_HEADS):
        sl = slice(h * RET_DV, (h + 1) * RET_DV)
        o = of_ref[:, sl] + ob_ref[:, sl]
        mu = jnp.mean(o, axis=-1, keepdims=True)
        d = o - mu
        var = jnp.mean(d * d, axis=-1, keepdims=True)
        y = d * lax.rsqrt(var + EPS) * gnw_ref[:, sl] + gnb_ref[:, sl]
        g_ref = g0_ref if h < 2 else g1_ref
        g = g_ref[:, (h % 2) * RET_DV:(h % 2 + 1) * RET_DV].astype(F32)
        acc = acc + _dot((y * _silu(g)).astype(BF16), w_ref[sl, :])
    o_ref[...] = x_ref[...] + mod_ref[2] * acc


def _ret_out(o_f, o_b, p, xs, mod, gn_w, gn_b, w_bf, rows):
    tm = rows.tm
    return pl.pallas_call(
        _ret_out_kernel,
        out_shape=jax.ShapeDtypeStruct(xs.shape, F32),
        grid=(rows.n,),
        in_specs=[
            pl.BlockSpec((tm, RET_V), rows.blk()),
            pl.BlockSpec((tm, RET_V), rows.blk()),
            pl.BlockSpec((tm, 1024), rows.blk(col=1)),
            pl.BlockSpec((tm, 1024), rows.blk(col=2)),
            pl.BlockSpec((tm, D_MODEL), rows.blk()),
            _mod_spec(rows),
            pl.BlockSpec((1, RET_V), lambda t: (0, 0)),
            pl.BlockSpec((1, RET_V), lambda t: (0, 0)),
            pl.BlockSpec((RET_V, D_MODEL), lambda t: (0, 0)),
        ],
        out_specs=pl.BlockSpec((tm, D_MODEL), rows.blk()),
        compiler_params=_cparams(("arbitrary",)),
        name="ret_out",
    )(o_f, o_b, p, p, xs, mod, gn_w, gn_b, w_bf)


def _gla_out_kernel(of_ref, ob_ref, r0_ref, r1_ref, x_ref, mod_ref, ng_ref, w_ref, o_ref):
    acc = jnp.zeros(o_ref.shape, F32)
    for h in range(GLA_HEADS):
        sl = slice(h * GLA_DV, (h + 1) * GLA_DV)
        o = of_ref[:, sl] + ob_ref[:, sl]
        y = o * lax.rsqrt(jnp.mean(o * o, axis=-1, keepdims=True) + EPS) * ng_ref[:, sl]
        r_ref = r0_ref if h < 2 else r1_ref
        r = r_ref[:, (h % 2) * GLA_DV:(h % 2 + 1) * GLA_DV].astype(F32)
        acc = acc + _dot((y * _silu(r)).astype(BF16), w_ref[sl, :])
    o_ref[...] = x_ref[...] + mod_ref[2] * acc


def _gla_out(o_f, o_b, p, xs, mod, norm_g, w_bf, rows, x_base_tile, out_rows):
    tm = rows.tm
    return pl.pallas_call(
        _gla_out_kernel,
        out_shape=jax.ShapeDtypeStruct((out_rows, D_MODEL), F32),
        grid=(rows.n,),
        in_specs=[
            pl.BlockSpec((tm, GLA_V), rows.blk()),
            pl.BlockSpec((tm, GLA_V), rows.blk()),
            pl.BlockSpec((tm, 512), rows.blk(col=1)),
            pl.BlockSpec((tm, 512), rows.blk(col=2)),
            pl.BlockSpec((tm, D_MODEL), rows.blk()),
            _mod_spec(rows),
            pl.BlockSpec((1, GLA_V), lambda t: (0, 0)),
            pl.BlockSpec((GLA_V, D_MODEL), lambda t: (0, 0)),
        ],
        out_specs=pl.BlockSpec((tm, D_MODEL), rows.blk(base_tile=x_base_tile)),
        compiler_params=_cparams(("arbitrary",)),
        name="gla_out",
    )(o_f, o_b, p, p, xs, mod, norm_g, w_bf)


def _ffn_kernel(x_ref, mod_ref, gain_ref, wg_ref, wu_ref, wd_ref, o_ref, h_scr, acc_scr):
    f = pl.program_id(1)

    @pl.when(f == 0)
    def _():
        h_scr[...] = _norm_mod(x_ref[...], gain_ref[...], mod_ref, 3, 4).astype(BF16)
        acc_scr[...] = jnp.zeros_like(acc_scr)

    h = h_scr[...]
    hid = (_silu(_dot(h, wg_ref[...])) * _dot(h, wu_ref[...])).astype(BF16)
    acc_scr[...] += _dot(hid, wd_ref[...])

    @pl.when(f == pl.num_programs(1) - 1)
    def _():
        o_ref[...] = x_ref[...] + mod_ref[5] * acc_scr[...]


def _ffn(xs, mod, gain, wg_bf, wu_bf, wd_bf, rows):
    tm, tf = rows.tm, D_FF // 2
    return pl.pallas_call(
        _ffn_kernel,
        out_shape=jax.ShapeDtypeStruct(xs.shape, F32),
        grid=(rows.n, D_FF // tf),
        in_specs=[
            pl.BlockSpec((tm, D_MODEL), rows.blk()),
            _mod_spec(rows),
            pl.BlockSpec((1, D_MODEL), lambda t, f: (0, 0)),
            pl.BlockSpec((D_MODEL, tf), lambda t, f: (0, f)),
            pl.BlockSpec((D_MODEL, tf), lambda t, f: (0, f)),
            pl.BlockSpec((tf, D_MODEL), lambda t, f: (f, 0)),
        ],
        out_specs=pl.BlockSpec((tm, D_MODEL), rows.blk()),
        scratch_shapes=[pltpu.VMEM((tm, D_MODEL), BF16), pltpu.VMEM((tm, D_MODEL), F32)],
        compiler_params=_cparams(("arbitrary", "arbitrary")),
        name="ffn",
    )(xs, mod, gain, wg_bf, wu_bf, wd_bf)


def _router_kernel(x_ref, mod_ref, gain_ref, wr_hi_ref, wr_lo_ref, h_ref, idx_ref, wt_ref):
    h = _norm_mod(x_ref[...], gain_ref[...], mod_ref, 3, 4)
    h_ref[...] = h
    h_hi = h.astype(BF16)
    h_lo = (h - h_hi.astype(F32)).astype(BF16)
    logits = _dot(h_hi, wr_hi_ref[...]) + _dot(h_lo, wr_hi_ref[...]) + _dot(h_hi, wr_lo_ref[...])
    lane = lax.broadcasted_iota(jnp.int32, logits.shape, 1)
    lane_f = lane.astype(F32)
    neg = jnp.float32(-jnp.inf)
    l1 = jnp.where(lane < N_EXPERTS, logits, neg)
    m1 = jnp.max(l1, axis=-1, keepdims=True)
    i1 = jnp.min(jnp.where(l1 == m1, lane_f, 128.0), axis=-1, keepdims=True)
    l2 = jnp.where(lane_f == i1, neg, l1)
    m2 = jnp.max(l2, axis=-1, keepdims=True)
    i2 = jnp.min(jnp.where(l2 == m2, lane_f, 128.0), axis=-1, keepdims=True)
    e2 = jnp.exp(m2 - m1)
    den = 1.0 + e2
    idx_ref[...] = jnp.where(lane == 0, i1, jnp.where(lane == 1, i2, 0.0)).astype(jnp.int32)
    wt_ref[...] = jnp.where(lane == 0, 1.0 / den, jnp.where(lane == 1, e2 / den, 0.0))


def _router(xs, mod, gain, wr_hi, wr_lo, rows, x_base_tile):
    tm = rows.tm
    nrow = rows.n * tm
    out_blk = lambda t: (t, 0)
    return pl.pallas_call(
        _router_kernel,
        out_shape=(jax.ShapeDtypeStruct((nrow, D_MODEL), F32),
                   jax.ShapeDtypeStruct((nrow, 128), jnp.int32),
                   jax.ShapeDtypeStruct((nrow, 128), F32)),
        grid=(rows.n,),
        in_specs=[
            pl.BlockSpec((tm, D_MODEL), rows.blk(base_tile=x_base_tile)),
            _mod_spec(rows),
            pl.BlockSpec((1, D_MODEL), lambda t: (0, 0)),
            pl.BlockSpec((D_MODEL, 128), lambda t: (0, 0)),
            pl.BlockSpec((D_MODEL, 128), lambda t: (0, 0)),
        ],
        out_specs=(pl.BlockSpec((tm, D_MODEL), out_blk),
                   pl.BlockSpec((tm, 128), out_blk),
                   pl.BlockSpec((tm, 128), out_blk)),
        compiler_params=_cparams(("arbitrary",)),
        name="moe_router",
    )(xs, mod, gain, wr_hi, wr_lo)


MOE_TM = 512
MOE_TF = 512


def _expert_kernel(te_ref, ta_ref, tok_ref, h_ref, wg_ref, wu_ref, wd_ref, o_ref,
                   hbuf, hb_scr, acc_scr, sem):
    t = pl.program_id(0)
    f = pl.program_id(1)
    nt = pl.num_programs(0)
    active = ta_ref[t] == 1

    def start_tile(tile, slot):
        base = tile * MOE_TM

        def body(r, carry):
            tok = tok_ref[base + r]
            pltpu.make_async_copy(h_ref.at[pl.ds(tok, 1)], hbuf.at[slot, pl.ds(r, 1)],
                                  sem.at[slot]).start()
            return carry
        lax.fori_loop(0, MOE_TM, body, 0, unroll=8)

    def wait_tile(slot):
        pltpu.make_async_copy(h_ref.at[pl.ds(0, MOE_TM)], hbuf.at[slot], sem.at[slot]).wait()

    @pl.when(f == 0)
    def _():
        slot = t % 2

        @pl.when(jnp.logical_and(t == 0, active))
        def _():
            start_tile(0, 0)

        nxt = jnp.minimum(t + 1, nt - 1)

        @pl.when(jnp.logical_and(t + 1 < nt, ta_ref[nxt] == 1))
        def _():
            start_tile(t + 1, 1 - slot)

        @pl.when(active)
        def _():
            wait_tile(slot)
            hb_scr[...] = hbuf[slot].astype(BF16)

        acc_scr[...] = jnp.zeros_like(acc_scr)

    @pl.when(active)
    def _():
        h = hb_scr[...]
        hid = (_silu(_dot(h, wg_ref[...])) * _dot(h, wu_ref[...])).astype(BF16)
        acc_scr[...] += _dot(hid, wd_ref[...])

    @pl.when(f == pl.num_programs(1) - 1)
    def _():
        o_ref[...] = acc_scr[...]


def _experts(tile_expert, tile_active, tok_of_slot, h, wg_bf, wu_bf, wd_bf):
    nslot = tok_of_slot.shape[0]
    nf = D_FF_EXPERT // MOE_TF

    def fcol(t, f, ta):
        return jnp.where(ta[t] == 1, f, nf - 1)
    return pl.pallas_call(
        _expert_kernel,
        out_shape=jax.ShapeDtypeStruct((nslot, D_MODEL), F32),
        grid_spec=pltpu.PrefetchScalarGridSpec(
            num_scalar_prefetch=3,
            grid=(nslot // MOE_TM, nf),
            in_specs=[
                pl.BlockSpec(memory_space=pl.ANY),
                pl.BlockSpec((None, D_MODEL, MOE_TF), lambda t, f, te, ta, tk: (te[t], 0, fcol(t, f, ta))),
                pl.BlockSpec((None, D_MODEL, MOE_TF), lambda t, f, te, ta, tk: (te[t], 0, fcol(t, f, ta))),
                pl.BlockSpec((None, MOE_TF, D_MODEL), lambda t, f, te, ta, tk: (te[t], fcol(t, f, ta), 0)),
            ],
            out_specs=pl.BlockSpec((MOE_TM, D_MODEL), lambda t, f, te, ta, tk: (t, 0)),
            scratch_shapes=[pltpu.VMEM((2, MOE_TM, D_MODEL), F32),
                            pltpu.VMEM((MOE_TM, D_MODEL), BF16),
                            pltpu.VMEM((MOE_TM, D_MODEL), F32),
                            pltpu.SemaphoreType.DMA((2,))],
        ),
        compiler_params=_cparams(("arbitrary", "arbitrary")),
        name="moe_experts",
    )(tile_expert, tile_active, tok_of_slot, h, wg_bf, wu_bf, wd_bf)


COMBINE_TM = 256


def _combine_kernel(pos_ref, ys_ref, x_ref, wt_ref, mod_ref, fg_ref, o_ref, buf, sem, *, final):
    t = pl.program_id(0)
    nt = pl.num_programs(0)
    slot = t % 2

    def start_tile(tile, sl):
        base = tile * COMBINE_TM

        def body(r, carry):
            for k in range(2):
                p = pos_ref[2 * (base + r) + k]
                pltpu.make_async_copy(ys_ref.at[pl.ds(p, 1)], buf.at[sl, k, pl.ds(r, 1)],
                                      sem.at[sl]).start()
            return carry
        lax.fori_loop(0, COMBINE_TM, body, 0, unroll=4)

    @pl.when(t == 0)
    def _():
        start_tile(0, 0)

    @pl.when(t + 1 < nt)
    def _():
        start_tile(t + 1, 1 - slot)

    for k in range(2):
        pltpu.make_async_copy(ys_ref.at[pl.ds(0, COMBINE_TM)], buf.at[slot, k], sem.at[slot]).wait()
    w = wt_ref[...]
    f = w[:, 0:1] * buf[slot, 0] + w[:, 1:2] * buf[slot, 1]
    y = x_ref[...] + mod_ref[5] * f
    if final:
        ms = jnp.mean(y * y, axis=-1, keepdims=True)
        y = y * lax.rsqrt(ms + EPS) * fg_ref[...]
    o_ref[...] = y


def _combine(pos, ys, xs, wts, mod, final_g, rows, x_base_tile, final):
    tm = rows.tm
    nrow = rows.n * tm
    return pl.pallas_call(
        functools.partial(_combine_kernel, final=final),
        out_shape=jax.ShapeDtypeStruct((nrow, D_MODEL), F32),
        grid_spec=pltpu.PrefetchScalarGridSpec(
            num_scalar_prefetch=1,
            grid=(rows.n,),
            in_specs=[
                pl.BlockSpec(memory_space=pl.ANY),
                pl.BlockSpec((tm, D_MODEL), rows.blk(base_tile=x_base_tile)),
                pl.BlockSpec((tm, 128), lambda t, p: (t, 0)),
                _mod_spec(rows),
                pl.BlockSpec((1, D_MODEL), lambda t, p: (0, 0)),
            ],
            out_specs=pl.BlockSpec((tm, D_MODEL), lambda t, p: (t, 0)),
            scratch_shapes=[pltpu.VMEM((2, 2, tm, D_MODEL), F32), pltpu.SemaphoreType.DMA((2,))],
        ),
        compiler_params=_cparams(("arbitrary",)),
        name="moe_combine",
    )(pos, ys, xs, wts, mod, final_g)


def _routing_tables(idx, nslot_pad):
    nrow = idx.shape[0]
    e_flat = idx[:, :2].reshape(-1)
    onehot = (e_flat[:, None] == jnp.arange(N_EXPERTS, dtype=jnp.int32)[None, :]).astype(jnp.int32)
    csum = jnp.cumsum(onehot, axis=0)
    rank = jnp.sum(csum * onehot, axis=1) - 1
    counts = csum[-1]
    padded = ((counts + MOE_TM - 1) // MOE_TM) * MOE_TM
    ends = jnp.cumsum(padded)
    offs = ends - padded
    pos = (jnp.sum(onehot * offs[None, :], axis=1) + rank).astype(jnp.int32)
    tok = jnp.arange(2 * nrow, dtype=jnp.int32) // 2
    tok_of_slot = jnp.zeros((nslot_pad,), jnp.int32).at[pos].set(tok)
    ntile = nslot_pad // MOE_TM
    tile_start = jnp.arange(ntile, dtype=jnp.int32) * MOE_TM
    tile_active = (tile_start < ends[-1]).astype(jnp.int32)
    last_start = jnp.maximum(ends[-1] - MOE_TM, 0)
    tile_expert = jnp.sum((jnp.minimum(tile_start, last_start)[:, None] >= ends[None, :]).astype(jnp.int32),
                          axis=1)
    tile_expert = jnp.minimum(tile_expert, N_EXPERTS - 1).astype(jnp.int32)
    return pos, tok_of_slot, tile_expert, tile_active


def _moe(xs, mod, gain, wr, wg_bf, wu_bf, wd_bf, final_g, rows, x_base_tile, final):
    wr_pad = jnp.zeros((D_MODEL, 128), F32).at[:, :N_EXPERTS].set(wr)
    wr_hi = wr_pad.astype(BF16)
    wr_lo = (wr_pad - wr_hi.astype(F32)).astype(BF16)
    h, idx, wts = _router(xs, mod, gain, wr_hi, wr_lo, rows, x_base_tile)
    nrow = h.shape[0]
    nslot_pad = 2 * nrow + N_EXPERTS * MOE_TM
    pos, tok_of_slot, tile_expert, tile_active = _routing_tables(idx, nslot_pad)
    ys = _experts(tile_expert, tile_active, tok_of_slot, h, wg_bf, wu_bf, wd_bf)
    return _combine(pos, ys, xs, wts, mod, final_g, rows, x_base_tile, final)


def kernel(x, c, ctx, c_ctx, ada_w, ada_b, norm_mix_g, norm_ffn_g, final_g, ret_w_in, ret_log_decay,
           ret_gn_w, ret_gn_b, ret_w_out, gla_w_in, gla_w_gate_up, gla_b_gate, gla_norm_g, gla_w_out,
           ffn_w_gate, ffn_w_up, ffn_w_down, moe_w_router, moe_w_gate, moe_w_up, moe_w_down):
    batch, seq, d = x.shape
    assert d == D_MODEL and ctx.shape == (batch, CTX_LEN, d) and seq % 1024 == 0
    nctx = batch * CTX_LEN
    xs = jnp.concatenate([ctx.reshape(nctx, d), x.reshape(batch * seq, d)], axis=0)
    mod_all = _modulation(c, c_ctx, ada_w, ada_b)

    proj_tm = 1024
    cos, sin = _rope_tables(seq, proj_tm)
    ret_colscale = jnp.concatenate([
        jnp.ones((1, RET_QK + RET_V), F32),
        jnp.full((1, RET_QK), RET_DK ** -0.5, F32),
        jnp.ones((1, RET_V), F32)], axis=1)
    gla_main = 2 * GLA_K + 2 * GLA_V
    gla_colscale = jnp.concatenate([
        jnp.full((1, GLA_K), GLA_DK ** -0.5, F32),
        jnp.ones((1, gla_main - GLA_K), F32)], axis=1)

    final_row = final_g.reshape(1, d)
    for i in range(DEPTH):
        last = i == DEPTH - 1
        j = i // 2
        mod = mod_all[i]
        rows_all_proj = _Rows(batch, seq, proj_tm, False)
        rows_all = _Rows(batch, seq, 512, False)
        rows_lat = _Rows(batch, seq, 512, True)
        mix_gain = norm_mix_g[i].reshape(1, d)
        ffn_gain = norm_ffn_g[i].reshape(1, d)
        if i % 2 == 0:
            p = _ret_proj(xs, mod, mix_gain, ret_w_in[j].astype(BF16), ret_colscale, cos, sin,
                          rows_all_proj)
            ld = jnp.broadcast_to(ret_log_decay[j].reshape(2 * RET_HEADS, 1, 1), (2 * RET_HEADS, 1, 128))
            o_f, o_b = _ret_scan(p, ld, batch, seq)
            xs = _ret_out(o_f, o_b, p, xs, mod, ret_gn_w[j].reshape(1, RET_V),
                          ret_gn_b[j].reshape(1, RET_V), ret_w_out[j].astype(BF16), rows_all)
            xs = _ffn(xs, mod, ffn_gain, ffn_w_gate[j].astype(BF16), ffn_w_up[j].astype(BF16),
                      ffn_w_down[j].astype(BF16), rows_all)
        else:
            w_in = gla_w_in[j]
            wa = jnp.zeros((d, 128), F32).at[:, :2 * GLA_GATE_RANK].set(w_in[:, gla_main:])
            wbd = jnp.zeros((128, 2 * GLA_K), F32)
            wbd = wbd.at[:GLA_GATE_RANK, :GLA_K].set(gla_w_gate_up[j, 0])
            wbd = wbd.at[GLA_GATE_RANK:2 * GLA_GATE_RANK, GLA_K:].set(gla_w_gate_up[j, 1])
            p, lg = _gla_proj(xs, mod, mix_gain, w_in[:, :gla_main].astype(BF16), gla_colscale,
                              wa.astype(BF16), wbd.astype(BF16), gla_b_gate[j].reshape(1, 2 * GLA_K),
                              rows_all_proj)
            o_f, o_b = _gla_scan(p, lg, batch, seq)
            rows = rows_lat if last else rows_all
            base = rows.start if last else 0
            xs = _gla_out(o_f, o_b, p, xs, mod, gla_norm_g[j].reshape(1, GLA_V),
                          gla_w_out[j].astype(BF16), rows, base, rows.n * rows.tm)
            rows_c = _Rows(batch, seq, COMBINE_TM, last)
            base_c = rows_c.start if last else 0
            xs = _moe(xs, mod, ffn_gain, moe_w_router[j], moe_w_gate[j].astype(BF16),
                      moe_w_up[j].astype(BF16), moe_w_down[j].astype(BF16), final_row,
                      rows_c, base_c, last)
    return xs.reshape(batch, seq, d)
```

```python
import functools

import jax
import jax.numpy as jnp
from jax import lax
from jax.experimental import pallas as pl
from jax.experimental.pallas import tpu as pltpu

D_MODEL = 1024
GRID_W = 64
CTX_LEN = 256
DEPTH = 4
RET_HEADS = 4
RET_DK = 256
RET_DV = 512
RET_QK = 1024
RET_V = 2048
GLA_HEADS = 4
GLA_DK = 128
GLA_DV = 256
GLA_K = 512
GLA_V = 1024
GLA_GATE_RANK = 16
GLA_TAU = 16.0
D_FF = 2816
N_EXPERTS = 8
D_FF_EXPERT = 3584
ROPE_BASE = 10000.0
EPS = 1e-6

BF16 = jnp.bfloat16
F32 = jnp.float32

SCAN_ROWS = 256
GLA_SUB = 128
GLA_BLK = 16
VMEM_LIMIT = 48 * 1024 * 1024


def _cparams(sem):
    return pltpu.CompilerParams(dimension_semantics=sem, vmem_limit_bytes=VMEM_LIMIT)


def _dot(a, b):
    return jnp.dot(a, b, preferred_element_type=F32)


def _dot_nt(a, b):
    return lax.dot_general(a, b, (((1,), (1,)), ((), ())), preferred_element_type=F32)


def _dot_tn(a, b):
    return lax.dot_general(a, b, (((0,), (0,)), ((), ())), preferred_element_type=F32)


def _silu(x):
    return x * jax.nn.sigmoid(x)


def _norm_mod(x, gain, mod_ref, shift_i, scale_i):
    ms = jnp.mean(x * x, axis=-1, keepdims=True)
    y = x * lax.rsqrt(ms + EPS) * gain
    return y * (1.0 + mod_ref[scale_i]) + mod_ref[shift_i]


class _Rows:
    def __init__(self, batch, seq, tm, lat_only):
        self.tm = tm
        self.nct = batch * CTX_LEN // tm
        self.tpb = seq // tm
        self.start = self.nct if lat_only else 0
        self.n = batch * seq // tm + (0 if lat_only else self.nct)
        self.batch = batch

    def blk(self, base_tile=0, col=0):
        off = self.start - base_tile
        return lambda t, *_: (t + off, col)

    def mod_blk(self):
        start, nct, tpb, batch = self.start, self.nct, self.tpb, self.batch

        def im(t, *_):
            tg = t + start
            return (jnp.where(tg < nct, batch, (tg - nct) // tpb), 0, 0, 0)
        return im

    def pos_blk(self):
        start, nct, tpb = self.start, self.nct, self.tpb

        def im(t, *_):
            tg = t + start
            return (jnp.where(tg < nct, 0, 1 + (tg - nct) % tpb), 0)
        return im


def _mod_spec(rows):
    return pl.BlockSpec((None, 6, 1, D_MODEL), rows.mod_blk())


def _mod_kernel(c_ref, w_ref, b_ref, o_ref):
    s = _silu(c_ref[...]).astype(BF16)
    o_ref[...] = _dot(s, w_ref[...].astype(BF16)) + b_ref[...]


def _modulation(c, c_ctx, ada_w, ada_b):
    batch = c.shape[0]
    nrow = 16
    cc = jnp.concatenate([c, c_ctx[None], jnp.zeros((nrow - batch - 1, D_MODEL), F32)], axis=0)
    out = pl.pallas_call(
        _mod_kernel,
        out_shape=jax.ShapeDtypeStruct((DEPTH, 6, nrow, D_MODEL), F32),
        grid=(DEPTH, 6),
        in_specs=[
            pl.BlockSpec((nrow, D_MODEL), lambda i, k: (0, 0)),
            pl.BlockSpec((None, D_MODEL, D_MODEL), lambda i, k: (i, 0, k)),
            pl.BlockSpec((None, None, 1, D_MODEL), lambda i, k: (i, k, 0, 0)),
        ],
        out_specs=pl.BlockSpec((None, None, nrow, D_MODEL), lambda i, k: (i, k, 0, 0)),
        compiler_params=_cparams(("arbitrary", "arbitrary")),
        name="adaln_mod",
    )(cc, ada_w, ada_b.reshape(DEPTH, 6, 1, D_MODEL))
    return out.transpose(0, 2, 1, 3)[:, :, :, None, :]


def _ret_proj_kernel(x_ref, mod_ref, gain_ref, w_ref, cs_ref, cos_ref, sin_ref, o_ref, h_scr):
    j = pl.program_id(1)

    @pl.when(j == 0)
    def _():
        h_scr[...] = _norm_mod(x_ref[...], gain_ref[...], mod_ref, 0, 1).astype(BF16)

    acc = _dot(h_scr[...], w_ref[...]) * cs_ref[...]
    is_rope = jnp.logical_or(j == 0, j == 3)

    @pl.when(is_rope)
    def _():
        for blk in range(D_MODEL // 128):
            lo = blk * 128
            tl = (blk % 2) * 128
            a = acc[:, lo:lo + 128]
            rot = pltpu.roll(a, 64, 1)
            o_ref[:, lo:lo + 128] = (a * cos_ref[:, tl:tl + 128]
                                     + rot * sin_ref[:, tl:tl + 128]).astype(BF16)

    @pl.when(jnp.logical_not(is_rope))
    def _():
        o_ref[...] = acc.astype(BF16)


def _rope_tables(seq, tm):
    half = RET_DK // 4
    freqs = ROPE_BASE ** (-jnp.arange(half, dtype=F32) / half)
    rows = seq // GRID_W
    row = jnp.broadcast_to(jnp.arange(rows, dtype=F32)[:, None], (rows, GRID_W)).reshape(seq)
    col = jnp.broadcast_to(jnp.arange(GRID_W, dtype=F32)[None, :], (rows, GRID_W)).reshape(seq)
    ar = row[:, None] * freqs[None, :]
    ac = col[:, None] * freqs[None, :]
    cos = jnp.concatenate([jnp.cos(ar), jnp.cos(ar), jnp.cos(ac), jnp.cos(ac)], axis=-1)
    sin = jnp.concatenate([-jnp.sin(ar), jnp.sin(ar), -jnp.sin(ac), jnp.sin(ac)], axis=-1)
    cos = jnp.concatenate([jnp.ones((tm, RET_DK), F32), cos], axis=0)
    sin = jnp.concatenate([jnp.zeros((tm, RET_DK), F32), sin], axis=0)
    return cos, sin


def _ret_proj(xs, mod, gain, w_bf, colscale, cos, sin, rows):
    tm, tn = rows.tm, 1024
    nout = w_bf.shape[1]
    return pl.pallas_call(
        _ret_proj_kernel,
        out_shape=jax.ShapeDtypeStruct((xs.shape[0], nout), BF16),
        grid=(rows.n, nout // tn),
        in_specs=[
            pl.BlockSpec((tm, D_MODEL), rows.blk()),
            _mod_spec(rows),
            pl.BlockSpec((1, D_MODEL), lambda t, j: (0, 0)),
            pl.BlockSpec((D_MODEL, tn), lambda t, j: (0, j)),
            pl.BlockSpec((1, tn), lambda t, j: (0, j)),
            pl.BlockSpec((tm, RET_DK), rows.pos_blk()),
            pl.BlockSpec((tm, RET_DK), rows.pos_blk()),
        ],
        out_specs=pl.BlockSpec((tm, tn), lambda t, j: (t + rows.start, j)),
        scratch_shapes=[pltpu.VMEM((tm, D_MODEL), BF16)],
        compiler_params=_cparams(("arbitrary", "arbitrary")),
        name="ret_proj",
    )(xs, mod, gain, w_bf, colscale, cos, sin)


def _log_sigmoid(z):
    return jnp.minimum(z, 0.0) - jnp.log1p(jnp.exp(-jnp.abs(z)))


def _gla_proj_kernel(x_ref, mod_ref, gain_ref, w_ref, cs_ref, wa_ref, wbd_ref, bg_ref,
                     o_ref, lg_ref, h_scr):
    j = pl.program_id(1)

    @pl.when(j == 0)
    def _():
        h = _norm_mod(x_ref[...], gain_ref[...], mod_ref, 0, 1).astype(BF16)
        h_scr[...] = h
        a = _dot(h, wa_ref[...]).astype(BF16)
        z = _dot(a, wbd_ref[...]) + bg_ref[...]
        lg_ref[...] = _log_sigmoid(z) * (1.0 / GLA_TAU)

    o_ref[...] = (_dot(h_scr[...], w_ref[...]) * cs_ref[...]).astype(BF16)


def _gla_proj(xs, mod, gain, w_bf, colscale, wa_bf, wbd_bf, bgate, rows):
    tm, tn = rows.tm, 1024
    nout = w_bf.shape[1]
    n = xs.shape[0]
    return pl.pallas_call(
        _gla_proj_kernel,
        out_shape=(jax.ShapeDtypeStruct((n, nout), BF16),
                   jax.ShapeDtypeStruct((n, 2 * GLA_K), F32)),
        grid=(rows.n, nout // tn),
        in_specs=[
            pl.BlockSpec((tm, D_MODEL), rows.blk()),
            _mod_spec(rows),
            pl.BlockSpec((1, D_MODEL), lambda t, j: (0, 0)),
            pl.BlockSpec((D_MODEL, tn), lambda t, j: (0, j)),
            pl.BlockSpec((1, tn), lambda t, j: (0, j)),
            pl.BlockSpec((D_MODEL, 128), lambda t, j: (0, 0)),
            pl.BlockSpec((128, 2 * GLA_K), lambda t, j: (0, 0)),
            pl.BlockSpec((1, 2 * GLA_K), lambda t, j: (0, 0)),
        ],
        out_specs=(pl.BlockSpec((tm, tn), lambda t, j: (t + rows.start, j)),
                   pl.BlockSpec((tm, 2 * GLA_K), lambda t, j: (t + rows.start, 0))),
        scratch_shapes=[pltpu.VMEM((tm, D_MODEL), BF16)],
        compiler_params=_cparams(("arbitrary", "arbitrary")),
        name="gla_proj",
    )(xs, mod, gain, w_bf, colscale, wa_bf, wbd_bf, bgate)


def _scan_row_maps(batch, seq):
    nlat = seq // SCAN_ROWS
    nctx_blocks = batch * CTX_LEN // SCAN_ROWS

    def fwd(b, c):
        return jnp.where(c == 0, b, nctx_blocks + b * nlat + c - 1)

    def bwd(b, c):
        return jnp.where(c == 0, b, nctx_blocks + b * nlat + nlat - c)
    return fwd, bwd, nlat + 1


def _ret_scan_kernel(ld_ref, qf_ref, kf_ref, vf_ref, qb_ref, kb_ref, vb_ref,
                     of_ref, ob_ref, sf_ref, sb_ref, dm_ref, qd_ref, kd_ref, cd_ref):
    b = pl.program_id(0)
    c = pl.program_id(1)
    n = SCAN_ROWS

    @pl.when(jnp.logical_and(b == 0, c == 0))
    def _():
        ii = lax.broadcasted_iota(jnp.int32, (n, n), 0).astype(F32)
        jj = lax.broadcasted_iota(jnp.int32, (n, n), 1).astype(F32)
        for d in range(2):
            rev = d == 1
            for h in range(RET_HEADS):
                lg = -jnp.exp(ld_ref[d * RET_HEADS + h])[:, :1]
                diff = (jj - ii) if rev else (ii - jj)
                dm_ref[d, h] = jnp.where(diff >= 0, jnp.exp(lg * jnp.maximum(diff, 0.0)), 0.0)
                qd_ref[d, h] = jnp.exp(lg * ((n - ii) if rev else (ii + 1.0)))
                kd_ref[d, h] = jnp.exp(lg * (ii if rev else (n - 1.0 - ii)))
                cd_ref[d, h] = jnp.broadcast_to(jnp.exp(lg * float(n)), (8, 128))

    @pl.when(c == 0)
    def _():
        sf_ref[...] = jnp.zeros_like(sf_ref)
        sb_ref[...] = jnp.zeros_like(sb_ref)

    def one(d, h, q_ref, k_ref, v_ref, o_ref, s_ref):
        qk = slice(h * RET_DK, (h + 1) * RET_DK)
        vv = slice(h * RET_DV, (h + 1) * RET_DV)
        q = q_ref[:, qk]
        k = k_ref[:, qk]
        v = v_ref[:, vv]
        scores = _dot_nt(q, k) * dm_ref[d, h]
        s = s_ref[h]
        o = _dot(scores.astype(BF16), v)
        o = o + _dot((q.astype(F32) * qd_ref[d, h]).astype(BF16), s.astype(BF16))
        o_ref[:, vv] = o
        s_ref[h] = s * cd_ref[d, h][:1, :1] + _dot_tn((k.astype(F32) * kd_ref[d, h]).astype(BF16), v)

    for h in range(RET_HEADS):
        one(0, h, qf_ref, kf_ref, vf_ref, of_ref, sf_ref)
        one(1, h, qb_ref, kb_ref, vb_ref, ob_ref, sb_ref)


def _ret_scan(p, ld, batch, seq):
    fwd, bwd, nsteps = _scan_row_maps(batch, seq)
    n = p.shape[0]
    kcol = (RET_QK + RET_V) // RET_QK
    vcol = (2 * RET_QK + RET_V) // RET_V
    R = SCAN_ROWS

    def specs(rm):
        return [
            pl.BlockSpec((R, RET_QK), lambda b, c: (rm(b, c), 0)),
            pl.BlockSpec((R, RET_QK), lambda b, c: (rm(b, c), kcol)),
            pl.BlockSpec((R, RET_V), lambda b, c: (rm(b, c), vcol)),
        ]
    state = pltpu.VMEM((RET_HEADS, RET_DK, RET_DV), F32)
    table = pltpu.VMEM((2, RET_HEADS, R, R), F32)
    return pl.pallas_call(
        _ret_scan_kernel,
        out_shape=(jax.ShapeDtypeStruct((n, RET_V), F32),
                   jax.ShapeDtypeStruct((n, RET_V), F32)),
        grid=(batch, nsteps),
        in_specs=[pl.BlockSpec((2 * RET_HEADS, 1, 128), lambda b, c: (0, 0, 0))]
        + specs(fwd) + specs(bwd),
        out_specs=(pl.BlockSpec((R, RET_V), lambda b, c: (fwd(b, c), 0)),
                   pl.BlockSpec((R, RET_V), lambda b, c: (bwd(b, c), 0))),
        scratch_shapes=[state, state, table, table, table,
                        pltpu.VMEM((2, RET_HEADS, 8, 128), F32)],
        compiler_params=_cparams(("arbitrary", "arbitrary")),
        name="ret_scan",
    )(ld, p, p, p, p, p, p)


def _gla_tri(rev):
    n = GLA_SUB
    ri = lax.broadcasted_iota(jnp.int32, (n, n), 0)
    ci = lax.broadcasted_iota(jnp.int32, (n, n), 1)
    return (ci >= ri) if rev else (ci <= ri)


def _gla_cum_gates(g_ref, i, rev):
    n = GLA_SUB
    g = g_ref[pl.ds(pl.multiple_of(i * n, n), n), :]
    tmat = jnp.where(_gla_tri(rev), 1.0, 0.0).astype(BF16)
    g_hi = g.astype(BF16)
    g_lo = (g - g_hi.astype(F32)).astype(BF16)
    return _dot(tmat, g_hi) + _dot(tmat, g_lo)


def _gla_sub_chunk(q_ref, k_ref, v_ref, b_all, o_ref, st_ref, h, i, rev):
    n, r = GLA_SUB, GLA_BLK
    nb = n // r
    rows = pl.ds(pl.multiple_of(i * n, n), n)
    kcols = slice(h * GLA_DK, (h + 1) * GLA_DK)
    vcols = slice(h * GLA_DV, (h + 1) * GLA_DV)
    q = q_ref[rows, kcols].astype(F32)
    k = k_ref[rows, kcols].astype(F32)
    v = v_ref[rows, vcols]
    tri = _gla_tri(rev)
    b = b_all[:, kcols]
    b_last = b[0:1] if rev else b[n - 1:n]
    st = st_ref[h]
    o = _dot_nt((q * jnp.exp(b)).astype(BF16), st.astype(BF16))
    k_out = (k * jnp.exp(b_last - b)).astype(BF16)
    st_ref[h] = st * jnp.exp(b_last) + _dot_tn(v, k_out)
    refs = [b[bi * r:bi * r + 1] if rev else b[bi * r + r - 1:bi * r + r] for bi in range(nb)]
    k_hat = jnp.concatenate(
        [k[bi * r:(bi + 1) * r] * jnp.exp(refs[bi] - b[bi * r:(bi + 1) * r]) for bi in range(nb)],
        axis=0).astype(BF16)
    groups = []
    where = {}
    for bj in range(nb):
        for bi in (range(bj, nb) if rev else range(bj + 1)):
            where[(bj, bi)] = len(groups)
            groups.append(q[bj * r:(bj + 1) * r] * jnp.exp(b[bj * r:(bj + 1) * r] - refs[bi]))
    q_cat = jnp.concatenate(groups, axis=0).astype(BF16)
    raw = _dot_nt(q_cat, k_hat)
    col_blk = lax.broadcasted_iota(jnp.int32, (r, n), 1) // r
    row_blocks = []
    for bj in range(nb):
        acc = jnp.zeros((r, n), F32)
        for bi in (range(bj, nb) if rev else range(bj + 1)):
            gi = where[(bj, bi)]
            acc = jnp.where(col_blk == bi, raw[gi * r:(gi + 1) * r], acc)
        row_blocks.append(acc)
    scores = jnp.where(tri, jnp.concatenate(row_blocks, axis=0), 0.0)
    o_ref[rows, vcols] = o + _dot(scores.astype(BF16), v)


def _gla_scan_kernel(qf_ref, kf_ref, vf_ref, gf_ref, qb_ref, kb_ref, vb_ref, gb_ref,
                     of_ref, ob_ref, sf_ref, sb_ref):
    c = pl.program_id(1)
    nsub = SCAN_ROWS // GLA_SUB

    @pl.when(c == 0)
    def _():
        sf_ref[...] = jnp.zeros_like(sf_ref)
        sb_ref[...] = jnp.zeros_like(sb_ref)

    def body(i, carry):
        ib = nsub - 1 - i
        b_f = _gla_cum_gates(gf_ref, i, False)
        b_b = _gla_cum_gates(gb_ref, ib, True)
        for h in range(GLA_HEADS):
            _gla_sub_chunk(qf_ref, kf_ref, vf_ref, b_f, of_ref, sf_ref, h, i, False)
            _gla_sub_chunk(qb_ref, kb_ref, vb_ref, b_b, ob_ref, sb_ref, h, ib, True)
        return carry

    lax.fori_loop(0, nsub, body, 0)


def _gla_scan(p, lg, batch, seq):
    fwd, bwd, nsteps = _scan_row_maps(batch, seq)
    n = p.shape[0]
    kcol = (GLA_K + GLA_V) // GLA_K
    vcol = (2 * GLA_K + GLA_V) // GLA_V
    R = SCAN_ROWS

    def specs(rm, gcol):
        return [
            pl.BlockSpec((R, GLA_K), lambda b, c: (rm(b, c), 0)),
            pl.BlockSpec((R, GLA_K), lambda b, c: (rm(b, c), kcol)),
            pl.BlockSpec((R, GLA_V), lambda b, c: (rm(b, c), vcol)),
            pl.BlockSpec((R, GLA_K), lambda b, c: (rm(b, c), gcol)),
        ]
    state = pltpu.VMEM((GLA_HEADS, GLA_DV, GLA_DK), F32)
    return pl.pallas_call(
        _gla_scan_kernel,
        out_shape=(jax.ShapeDtypeStruct((n, GLA_V), F32),
                   jax.ShapeDtypeStruct((n, GLA_V), F32)),
        grid=(batch, nsteps),
        in_specs=specs(fwd, 0) + specs(bwd, 1),
        out_specs=(pl.BlockSpec((R, GLA_V), lambda b, c: (fwd(b, c), 0)),
                   pl.BlockSpec((R, GLA_V), lambda b, c: (bwd(b, c), 0))),
        scratch_shapes=[state, state],
        compiler_params=_cparams(("arbitrary", "arbitrary")),
        name="gla_scan",
    )(p, p, p, lg, p, p, p, lg)


def _ret_out_kernel(of_ref, ob_ref, g0_ref, g1_ref, x_ref, mod_ref, gnw_ref, gnb_ref, w_ref, o_ref):
    acc = jnp.zeros(o_ref.shape, F32)
    for h in range(RET_HEADS):
        sl = slice(h * RET_DV, (h + 1) * RET_DV)
        o = of_ref[:, sl] + ob_ref[:, sl]
        mu = jnp.mean(o, axis=-1, keepdims=True)
        d = o - mu
        var = jnp.mean(d * d, axis=-1, keepdims=True)
        y = d * lax.rsqrt(var + EPS) * gnw_ref[:, sl] + gnb_ref[:, sl]
        g_ref = g0_ref if h < 2 else g1_ref
        g = g_ref[:, (h % 2) * RET_DV:(h % 2 + 1) * RET_DV].astype(F32)
        acc = acc + _dot((y * _silu(g)).astype(BF16), w_ref[sl, :])
    o_ref[...] = x_ref[...] + mod_ref[2] * acc


def _ret_out(o_f, o_b, p, xs, mod, gn_w, gn_b, w_bf, rows):
    tm = rows.tm
    return pl.pallas_call(
        _ret_out_kernel,
        out_shape=jax.ShapeDtypeStruct(xs.shape, F32),
        grid=(rows.n,),
        in_specs=[
            pl.BlockSpec((tm, RET_V), rows.blk()),
            pl.BlockSpec((tm, RET_V), rows.blk()),
            pl.BlockSpec((tm, 1024), rows.blk(col=1)),
            pl.BlockSpec((tm, 1024), rows.blk(col=2)),
            pl.BlockSpec((tm, D_MODEL), rows.blk()),
            _mod_spec(rows),
            pl.BlockSpec((1, RET_V), lambda t: (0, 0)),
            pl.BlockSpec((1, RET_V), lambda t: (0, 0)),
            pl.BlockSpec((RET_V, D_MODEL), lambda t: (0, 0)),
        ],
        out_specs=pl.BlockSpec((tm, D_MODEL), rows.blk()),
        compiler_params=_cparams(("arbitrary",)),
        name="ret_out",
    )(o_f, o_b, p, p, xs, mod, gn_w, gn_b, w_bf)


def _gla_out_kernel(of_ref, ob_ref, r0_ref, r1_ref, x_ref, mod_ref, ng_ref, w_ref, o_ref):
    acc = jnp.zeros(o_ref.shape, F32)
    for h in range(GLA_HEADS):
        sl = slice(h * GLA_DV, (h + 1) * GLA_DV)
        o = of_ref[:, sl] + ob_ref[:, sl]
        y = o * lax.rsqrt(jnp.mean(o * o, axis=-1, keepdims=True) + EPS) * ng_ref[:, sl]
        r_ref = r0_ref if h < 2 else r1_ref
        r = r_ref[:, (h % 2) * GLA_DV:(h % 2 + 1) * GLA_DV].astype(F32)
        acc = acc + _dot((y * _silu(r)).astype(BF16), w_ref[sl, :])
    o_ref[...] = x_ref[...] + mod_ref[2] * acc


def _gla_out(o_f, o_b, p, xs, mod, norm_g, w_bf, rows, x_base_tile, out_rows):
    tm = rows.tm
    return pl.pallas_call(
        _gla_out_kernel,
        out_shape=jax.ShapeDtypeStruct((out_rows, D_MODEL), F32),
        grid=(rows.n,),
        in_specs=[
            pl.BlockSpec((tm, GLA_V), rows.blk()),
            pl.BlockSpec((tm, GLA_V), rows.blk()),
            pl.BlockSpec((tm, 512), rows.blk(col=1)),
            pl.BlockSpec((tm, 512), rows.blk(col=2)),
            pl.BlockSpec((tm, D_MODEL), rows.blk()),
            _mod_spec(rows),
            pl.BlockSpec((1, GLA_V), lambda t: (0, 0)),
            pl.BlockSpec((GLA_V, D_MODEL), lambda t: (0, 0)),
        ],
        out_specs=pl.BlockSpec((tm, D_MODEL), rows.blk(base_tile=x_base_tile)),
        compiler_params=_cparams(("arbitrary",)),
        name="gla_out",
    )(o_f, o_b, p, p, xs, mod, norm_g, w_bf)


def _ffn_kernel(x_ref, mod_ref, gain_ref, wg_ref, wu_ref, wd_ref, o_ref, h_scr, acc_scr):
    f = pl.program_id(1)

    @pl.when(f == 0)
    def _():
        h_scr[...] = _norm_mod(x_ref[...], gain_ref[...], mod_ref, 3, 4).astype(BF16)
        acc_scr[...] = jnp.zeros_like(acc_scr)

    h = h_scr[...]
    hid = (_silu(_dot(h, wg_ref[...])) * _dot(h, wu_ref[...])).astype(BF16)
    acc_scr[...] += _dot(hid, wd_ref[...])

    @pl.when(f == pl.num_programs(1) - 1)
    def _():
        o_ref[...] = x_ref[...] + mod_ref[5] * acc_scr[...]


def _ffn(xs, mod, gain, wg_bf, wu_bf, wd_bf, rows):
    tm, tf = rows.tm, D_FF // 2
    return pl.pallas_call(
        _ffn_kernel,
        out_shape=jax.ShapeDtypeStruct(xs.shape, F32),
        grid=(rows.n, D_FF // tf),
        in_specs=[
            pl.BlockSpec((tm, D_MODEL), rows.blk()),
            _mod_spec(rows),
            pl.BlockSpec((1, D_MODEL), lambda t, f: (0, 0)),
            pl.BlockSpec((D_MODEL, tf), lambda t, f: (0, f)),
            pl.BlockSpec((D_MODEL, tf), lambda t, f: (0, f)),
            pl.BlockSpec((tf, D_MODEL), lambda t, f: (f, 0)),
        ],
        out_specs=pl.BlockSpec((tm, D_MODEL), rows.blk()),
        scratch_shapes=[pltpu.VMEM((tm, D_MODEL), BF16), pltpu.VMEM((tm, D_MODEL), F32)],
        compiler_params=_cparams(("arbitrary", "arbitrary")),
        name="ffn",
    )(xs, mod, gain, wg_bf, wu_bf, wd_bf)


def _router_kernel(x_ref, mod_ref, gain_ref, wr_hi_ref, wr_lo_ref, h_ref, idx_ref, wt_ref):
    h = _norm_mod(x_ref[...], gain_ref[...], mod_ref, 3, 4)
    h_ref[...] = h
    h_hi = h.astype(BF16)
    h_lo = (h - h_hi.astype(F32)).astype(BF16)
    logits = _dot(h_hi, wr_hi_ref[...]) + _dot(h_lo, wr_hi_ref[...]) + _dot(h_hi, wr_lo_ref[...])
    lane = lax.broadcasted_iota(jnp.int32, logits.shape, 1)
    lane_f = lane.astype(F32)
    neg = jnp.float32(-jnp.inf)
    l1 = jnp.where(lane < N_EXPERTS, logits, neg)
    m1 = jnp.max(l1, axis=-1, keepdims=True)
    i1 = jnp.min(jnp.where(l1 == m1, lane_f, 128.0), axis=-1, keepdims=True)
    l2 = jnp.where(lane_f == i1, neg, l1)
    m2 = jnp.max(l2, axis=-1, keepdims=True)
    i2 = jnp.min(jnp.where(l2 == m2, lane_f, 128.0), axis=-1, keepdims=True)
    e2 = jnp.exp(m2 - m1)
    den = 1.0 + e2
    idx_ref[...] = jnp.where(lane == 0, i1, jnp.where(lane == 1, i2, 0.0)).astype(jnp.int32)
    wt_ref[...] = jnp.where(lane == 0, 1.0 / den, jnp.where(lane == 1, e2 / den, 0.0))


def _router(xs, mod, gain, wr_hi, wr_lo, rows, x_base_tile):
    tm = rows.tm
    nrow = rows.n * tm
    out_blk = lambda t: (t, 0)
    return pl.pallas_call(
        _router_kernel,
        out_shape=(jax.ShapeDtypeStruct((nrow, D_MODEL), F32),
                   jax.ShapeDtypeStruct((nrow, 128), jnp.int32),
                   jax.ShapeDtypeStruct((nrow, 128), F32)),
        grid=(rows.n,),
        in_specs=[
            pl.BlockSpec((tm, D_MODEL), rows.blk(base_tile=x_base_tile)),
            _mod_spec(rows),
            pl.BlockSpec((1, D_MODEL), lambda t: (0, 0)),
            pl.BlockSpec((D_MODEL, 128), lambda t: (0, 0)),
            pl.BlockSpec((D_MODEL, 128), lambda t: (0, 0)),
        ],
        out_specs=(pl.BlockSpec((tm, D_MODEL), out_blk),
                   pl.BlockSpec((tm, 128), out_blk),
                   pl.BlockSpec((tm, 128), out_blk)),
        compiler_params=_cparams(("arbitrary",)),
        name="moe_router",
    )(xs, mod, gain, wr_hi, wr_lo)


MOE_TM = 512
MOE_TF = 512


def _expert_kernel(te_ref, ta_ref, tok_ref, h_ref, wg_ref, wu_ref, wd_ref, o_ref,
                   hbuf, hb_scr, acc_scr, sem):
    t = pl.program_id(0)
    f = pl.program_id(1)
    nt = pl.num_programs(0)
    active = ta_ref[t] == 1

    nf = D_FF_EXPERT // MOE_TF
    per_step = MOE_TM // nf
    slot = t % 2

    def start_row(tile, sl, r):
        tok = tok_ref[tile * MOE_TM + r]
        pltpu.make_async_copy(h_ref.at[pl.ds(tok, 1)], hbuf.at[sl, pl.ds(r, 1)], sem.at[sl]).start()

    @pl.when(f == 0)
    def _():
        @pl.when(t == 0)
        def _():
            def body(r, carry):
                start_row(0, 0, r)
                return carry
            lax.fori_loop(0, MOE_TM, body, 0, unroll=8)

        requested = jnp.logical_or(t == 0, ta_ref[jnp.maximum(t - 1, 0)] == 1)

        @pl.when(requested)
        def _():
            pltpu.make_async_copy(h_ref.at[pl.ds(0, MOE_TM)], hbuf.at[slot], sem.at[slot]).wait()

        @pl.when(active)
        def _():
            hb_scr[...] = hbuf[slot].astype(BF16)
            for r in range(per_step * nf, MOE_TM):
                start_row(t + 1, 1 - slot, r)

        acc_scr[...] = jnp.zeros_like(acc_scr)

    @pl.when(active)
    def _():
        for r in range(per_step):
            start_row(t + 1, 1 - slot, f * per_step + r)
        h = hb_scr[...]
        hid = (_silu(_dot(h, wg_ref[...])) * _dot(h, wu_ref[...])).astype(BF16)
        acc_scr[...] += _dot(hid, wd_ref[...])

    @pl.when(f == pl.num_programs(1) - 1)
    def _():
        o_ref[...] = acc_scr[...]


def _experts(tile_expert, tile_active, tok_of_slot, h, wg_bf, wu_bf, wd_bf, layer):
    nslot = tok_of_slot.shape[0]
    nf = D_FF_EXPERT // MOE_TF

    def fcol(t, f, ta):
        return jnp.where(ta[t] == 1, f, nf - 1)
    return pl.pallas_call(
        _expert_kernel,
        out_shape=jax.ShapeDtypeStruct((nslot, D_MODEL), F32),
        grid_spec=pltpu.PrefetchScalarGridSpec(
            num_scalar_prefetch=3,
            grid=(nslot // MOE_TM, nf),
            in_specs=[
                pl.BlockSpec(memory_space=pl.ANY),
                pl.BlockSpec((None, None, D_MODEL, MOE_TF),
                             lambda t, f, te, ta, tk: (layer, te[t], 0, fcol(t, f, ta))),
                pl.BlockSpec((None, None, D_MODEL, MOE_TF),
                             lambda t, f, te, ta, tk: (layer, te[t], 0, fcol(t, f, ta))),
                pl.BlockSpec((None, None, MOE_TF, D_MODEL),
                             lambda t, f, te, ta, tk: (layer, te[t], fcol(t, f, ta), 0)),
            ],
            out_specs=pl.BlockSpec((MOE_TM, D_MODEL), lambda t, f, te, ta, tk: (t, 0)),
            scratch_shapes=[pltpu.VMEM((2, MOE_TM, D_MODEL), F32),
                            pltpu.VMEM((MOE_TM, D_MODEL), BF16),
                            pltpu.VMEM((MOE_TM, D_MODEL), F32),
                            pltpu.SemaphoreType.DMA((2,))],
        ),
        compiler_params=_cparams(("arbitrary", "arbitrary")),
        name="moe_experts",
    )(tile_expert, tile_active, tok_of_slot, h, wg_bf, wu_bf, wd_bf)


COMBINE_TM = 256


def _combine_kernel(pos_ref, ys_ref, x_ref, wt_ref, mod_ref, fg_ref, o_ref, buf, sem, *, final):
    t = pl.program_id(0)
    nt = pl.num_programs(0)
    slot = t % 2

    def start_tile(tile, sl):
        base = tile * COMBINE_TM

        def body(r, carry):
            for k in range(2):
                p = pos_ref[2 * (base + r) + k]
                pltpu.make_async_copy(ys_ref.at[pl.ds(p, 1)], buf.at[sl, k, pl.ds(r, 1)],
                                      sem.at[sl]).start()
            return carry
        lax.fori_loop(0, COMBINE_TM, body, 0, unroll=4)

    @pl.when(t == 0)
    def _():
        start_tile(0, 0)

    @pl.when(t + 1 < nt)
    def _():
        start_tile(t + 1, 1 - slot)

    for k in range(2):
        pltpu.make_async_copy(ys_ref.at[pl.ds(0, COMBINE_TM)], buf.at[slot, k], sem.at[slot]).wait()
    w = wt_ref[...]
    f = w[:, 0:1] * buf[slot, 0] + w[:, 1:2] * buf[slot, 1]
    y = x_ref[...] + mod_ref[5] * f
    if final:
        ms = jnp.mean(y * y, axis=-1, keepdims=True)
        y = y * lax.rsqrt(ms + EPS) * fg_ref[...]
    o_ref[...] = y


def _combine(pos, ys, xs, wts, mod, final_g, rows, x_base_tile, final):
    tm = rows.tm
    nrow = rows.n * tm
    return pl.pallas_call(
        functools.partial(_combine_kernel, final=final),
        out_shape=jax.ShapeDtypeStruct((nrow, D_MODEL), F32),
        grid_spec=pltpu.PrefetchScalarGridSpec(
            num_scalar_prefetch=1,
            grid=(rows.n,),
            in_specs=[
                pl.BlockSpec(memory_space=pl.ANY),
                pl.BlockSpec((tm, D_MODEL), rows.blk(base_tile=x_base_tile)),
                pl.BlockSpec((tm, 128), lambda t, p: (t, 0)),
                _mod_spec(rows),
                pl.BlockSpec((1, D_MODEL), lambda t, p: (0, 0)),
            ],
            out_specs=pl.BlockSpec((tm, D_MODEL), lambda t, p: (t, 0)),
            scratch_shapes=[pltpu.VMEM((2, 2, tm, D_MODEL), F32), pltpu.SemaphoreType.DMA((2,))],
        ),
        compiler_params=_cparams(("arbitrary",)),
        name="moe_combine",
    )(pos, ys, xs, wts, mod, final_g)


def _routing_tables(idx, nslot_pad):
    nrow = idx.shape[0]
    e_flat = idx[:, :2].reshape(-1)
    onehot = (e_flat[:, None] == jnp.arange(N_EXPERTS, dtype=jnp.int32)[None, :]).astype(jnp.int32)
    csum = jnp.cumsum(onehot, axis=0)
    rank = jnp.sum(csum * onehot, axis=1) - 1
    counts = csum[-1]
    padded = ((counts + MOE_TM - 1) // MOE_TM) * MOE_TM
    ends = jnp.cumsum(padded)
    offs = ends - padded
    pos = (jnp.sum(onehot * offs[None, :], axis=1) + rank).astype(jnp.int32)
    tok = jnp.arange(2 * nrow, dtype=jnp.int32) // 2
    tok_of_slot = jnp.zeros((nslot_pad,), jnp.int32).at[pos].set(tok)
    ntile = nslot_pad // MOE_TM
    tile_start = jnp.arange(ntile, dtype=jnp.int32) * MOE_TM
    tile_active = (tile_start < ends[-1]).astype(jnp.int32)
    last_start = jnp.maximum(ends[-1] - MOE_TM, 0)
    tile_expert = jnp.sum((jnp.minimum(tile_start, last_start)[:, None] >= ends[None, :]).astype(jnp.int32),
                          axis=1)
    tile_expert = jnp.minimum(tile_expert, N_EXPERTS - 1).astype(jnp.int32)
    return pos, tok_of_slot, tile_expert, tile_active


def _moe(xs, mod, gain, wr, wg_bf, wu_bf, wd_bf, layer, final_g, rows, x_base_tile, final):
    wr_pad = jnp.zeros((D_MODEL, 128), F32).at[:, :N_EXPERTS].set(wr)
    wr_hi = wr_pad.astype(BF16)
    wr_lo = (wr_pad - wr_hi.astype(F32)).astype(BF16)
    h, idx, wts = _router(xs, mod, gain, wr_hi, wr_lo, rows, x_base_tile)
    nrow = h.shape[0]
    nslot_pad = 2 * nrow + N_EXPERTS * MOE_TM
    pos, tok_of_slot, tile_expert, tile_active = _routing_tables(idx, nslot_pad)
    ys = _experts(tile_expert, tile_active, tok_of_slot, h, wg_bf, wu_bf, wd_bf, layer)
    return _combine(pos, ys, xs, wts, mod, final_g, rows, x_base_tile, final)


def kernel(x, c, ctx, c_ctx, ada_w, ada_b, norm_mix_g, norm_ffn_g, final_g, ret_w_in, ret_log_decay,
           ret_gn_w, ret_gn_b, ret_w_out, gla_w_in, gla_w_gate_up, gla_b_gate, gla_norm_g, gla_w_out,
           ffn_w_gate, ffn_w_up, ffn_w_down, moe_w_router, moe_w_gate, moe_w_up, moe_w_down):
    batch, seq, d = x.shape
    assert d == D_MODEL and ctx.shape == (batch, CTX_LEN, d) and seq % 1024 == 0
    nctx = batch * CTX_LEN
    xs = jnp.concatenate([ctx.reshape(nctx, d), x.reshape(batch * seq, d)], axis=0)
    mod_all = _modulation(c, c_ctx, ada_w, ada_b)

    proj_tm = 1024
    cos, sin = _rope_tables(seq, proj_tm)
    ret_colscale = jnp.concatenate([
        jnp.ones((1, RET_QK + RET_V), F32),
        jnp.full((1, RET_QK), RET_DK ** -0.5, F32),
        jnp.ones((1, RET_V), F32)], axis=1)
    gla_main = 2 * GLA_K + 2 * GLA_V
    gla_colscale = jnp.concatenate([
        jnp.full((1, GLA_K), GLA_DK ** -0.5, F32),
        jnp.ones((1, gla_main - GLA_K), F32)], axis=1)

    final_row = final_g.reshape(1, d)
    moe_gate_bf = moe_w_gate.astype(BF16)
    moe_up_bf = moe_w_up.astype(BF16)
    moe_down_bf = moe_w_down.astype(BF16)
    for i in range(DEPTH):
        last = i == DEPTH - 1
        j = i // 2
        mod = mod_all[i]
        rows_all_proj = _Rows(batch, seq, proj_tm, False)
        rows_all = _Rows(batch, seq, 512, False)
        rows_lat = _Rows(batch, seq, 512, True)
        mix_gain = norm_mix_g[i].reshape(1, d)
        ffn_gain = norm_ffn_g[i].reshape(1, d)
        if i % 2 == 0:
            p = _ret_proj(xs, mod, mix_gain, ret_w_in[j].astype(BF16), ret_colscale, cos, sin,
                          rows_all_proj)
            ld = jnp.broadcast_to(ret_log_decay[j].reshape(2 * RET_HEADS, 1, 1), (2 * RET_HEADS, 1, 128))
            o_f, o_b = _ret_scan(p, ld, batch, seq)
            xs = _ret_out(o_f, o_b, p, xs, mod, ret_gn_w[j].reshape(1, RET_V),
                          ret_gn_b[j].reshape(1, RET_V), ret_w_out[j].astype(BF16), rows_all)
            xs = _ffn(xs, mod, ffn_gain, ffn_w_gate[j].astype(BF16), ffn_w_up[j].astype(BF16),
                      ffn_w_down[j].astype(BF16), rows_all)
        else:
            w_in = gla_w_in[j]
            wa = jnp.zeros((d, 128), F32).at[:, :2 * GLA_GATE_RANK].set(w_in[:, gla_main:])
            wbd = jnp.zeros((128, 2 * GLA_K), F32)
            wbd = wbd.at[:GLA_GATE_RANK, :GLA_K].set(gla_w_gate_up[j, 0])
            wbd = wbd.at[GLA_GATE_RANK:2 * GLA_GATE_RANK, GLA_K:].set(gla_w_gate_up[j, 1])
            p, lg = _gla_proj(xs, mod, mix_gain, w_in[:, :gla_main].astype(BF16), gla_colscale,
                              wa.astype(BF16), wbd.astype(BF16), gla_b_gate[j].reshape(1, 2 * GLA_K),
                              rows_all_proj)
            o_f, o_b = _gla_scan(p, lg, batch, seq)
            rows = rows_lat if last else rows_all
            base = rows.start if last else 0
            xs = _gla_out(o_f, o_b, p, xs, mod, gla_norm_g[j].reshape(1, GLA_V),
                          gla_w_out[j].astype(BF16), rows, base, rows.n * rows.tm)
            rows_c = _Rows(batch, seq, COMBINE_TM, last)
            base_c = rows_c.start if last else 0
            xs = _moe(xs, mod, ffn_gain, moe_w_router[j], moe_gate_bf, moe_up_bf, moe_down_bf, j,
                      final_row, rows_c, base_c, last)
    return xs.reshape(batch, seq, d)
```

```python
import functools

import jax
import jax.numpy as jnp
from jax import lax
from jax.experimental import pallas as pl
from jax.experimental.pallas import tpu as pltpu

D_MODEL = 1024
GRID_W = 64
CTX_LEN = 256
DEPTH = 4
RET_HEADS = 4
RET_DK = 256
RET_DV = 512
RET_QK = 1024
RET_V = 2048
GLA_HEADS = 4
GLA_DK = 128
GLA_DV = 256
GLA_K = 512
GLA_V = 1024
GLA_GATE_RANK = 16
GLA_TAU = 16.0
D_FF = 2816
N_EXPERTS = 8
D_FF_EXPERT = 3584
ROPE_BASE = 10000.0
EPS = 1e-6

BF16 = jnp.bfloat16
F32 = jnp.float32

SCAN_ROWS = 256
GLA_SUB = 128
GLA_BLK = 16
VMEM_LIMIT = 48 * 1024 * 1024


def _cparams(sem):
    return pltpu.CompilerParams(dimension_semantics=sem, vmem_limit_bytes=VMEM_LIMIT)


def _dot(a, b):
    return jnp.dot(a, b, preferred_element_type=F32)


def _dot_nt(a, b):
    return lax.dot_general(a, b, (((1,), (1,)), ((), ())), preferred_element_type=F32)


def _dot_tn(a, b):
    return lax.dot_general(a, b, (((0,), (0,)), ((), ())), preferred_element_type=F32)


def _silu(x):
    return x * jax.nn.sigmoid(x)


def _norm_mod(x, gain, mod_ref, shift_i, scale_i):
    ms = jnp.mean(x * x, axis=-1, keepdims=True)
    y = x * lax.rsqrt(ms + EPS) * gain
    return y * (1.0 + mod_ref[scale_i]) + mod_ref[shift_i]


class _Rows:
    def __init__(self, batch, seq, tm, lat_only):
        self.tm = tm
        self.nct = batch * CTX_LEN // tm
        self.tpb = seq // tm
        self.start = self.nct if lat_only else 0
        self.n = batch * seq // tm + (0 if lat_only else self.nct)
        self.batch = batch

    def blk(self, base_tile=0, col=0):
        off = self.start - base_tile
        return lambda t, *_: (t + off, col)

    def mod_blk(self):
        start, nct, tpb, batch = self.start, self.nct, self.tpb, self.batch

        def im(t, *_):
            tg = t + start
            return (jnp.where(tg < nct, batch, (tg - nct) // tpb), 0, 0, 0)
        return im

    def pos_blk(self):
        start, nct, tpb = self.start, self.nct, self.tpb

        def im(t, *_):
            tg = t + start
            return (jnp.where(tg < nct, 0, 1 + (tg - nct) % tpb), 0)
        return im


def _mod_spec(rows):
    return pl.BlockSpec((None, 6, 1, D_MODEL), rows.mod_blk())


def _mod_kernel(c_ref, w_ref, b_ref, o_ref):
    s = _silu(c_ref[...]).astype(BF16)
    o_ref[...] = _dot(s, w_ref[...].astype(BF16)) + b_ref[...]


def _modulation(c, c_ctx, ada_w, ada_b):
    batch = c.shape[0]
    nrow = 16
    cc = jnp.concatenate([c, c_ctx[None], jnp.zeros((nrow - batch - 1, D_MODEL), F32)], axis=0)
    out = pl.pallas_call(
        _mod_kernel,
        out_shape=jax.ShapeDtypeStruct((DEPTH, 6, nrow, D_MODEL), F32),
        grid=(DEPTH, 6),
        in_specs=[
            pl.BlockSpec((nrow, D_MODEL), lambda i, k: (0, 0)),
            pl.BlockSpec((None, D_MODEL, D_MODEL), lambda i, k: (i, 0, k)),
            pl.BlockSpec((None, None, 1, D_MODEL), lambda i, k: (i, k, 0, 0)),
        ],
        out_specs=pl.BlockSpec((None, None, nrow, D_MODEL), lambda i, k: (i, k, 0, 0)),
        compiler_params=_cparams(("arbitrary", "arbitrary")),
        name="adaln_mod",
    )(cc, ada_w, ada_b.reshape(DEPTH, 6, 1, D_MODEL))
    return out.transpose(0, 2, 1, 3)[:, :, :, None, :]


PROJ_TN = 1024
PROJ_RC = 256


def _ret_proj_kernel(x_ref, mod_ref, gain_ref, w_ref, cs_ref, cos_ref, sin_ref, o_ref):
    rope_tiles = (0, (RET_QK + RET_V) // PROJ_TN)
    for r in range(x_ref.shape[0] // PROJ_RC):
        rs = slice(r * PROJ_RC, (r + 1) * PROJ_RC)
        h = _norm_mod(x_ref[rs, :], gain_ref[...], mod_ref, 0, 1).astype(BF16)
        for j in range(w_ref.shape[1] // PROJ_TN):
            cols = slice(j * PROJ_TN, (j + 1) * PROJ_TN)
            acc = _dot(h, w_ref[:, cols]) * cs_ref[:, cols]
            if j not in rope_tiles:
                o_ref[rs, cols] = acc.astype(BF16)
                continue
            for blk in range(PROJ_TN // 128):
                lo = blk * 128
                tl = (blk % 2) * 128
                a = acc[:, lo:lo + 128]
                rot = pltpu.roll(a, 64, 1)
                o_ref[rs, j * PROJ_TN + lo:j * PROJ_TN + lo + 128] = (
                    a * cos_ref[rs, tl:tl + 128] + rot * sin_ref[rs, tl:tl + 128]).astype(BF16)


def _rope_tables(seq, tm):
    half = RET_DK // 4
    freqs = ROPE_BASE ** (-jnp.arange(half, dtype=F32) / half)
    rows = seq // GRID_W
    row = jnp.broadcast_to(jnp.arange(rows, dtype=F32)[:, None], (rows, GRID_W)).reshape(seq)
    col = jnp.broadcast_to(jnp.arange(GRID_W, dtype=F32)[None, :], (rows, GRID_W)).reshape(seq)
    ar = row[:, None] * freqs[None, :]
    ac = col[:, None] * freqs[None, :]
    cos = jnp.concatenate([jnp.cos(ar), jnp.cos(ar), jnp.cos(ac), jnp.cos(ac)], axis=-1)
    sin = jnp.concatenate([-jnp.sin(ar), jnp.sin(ar), -jnp.sin(ac), jnp.sin(ac)], axis=-1)
    cos = jnp.concatenate([jnp.ones((tm, RET_DK), F32), cos], axis=0)
    sin = jnp.concatenate([jnp.zeros((tm, RET_DK), F32), sin], axis=0)
    return cos, sin


def _resident(shape):
    return pl.BlockSpec(shape, lambda *_: (0,) * len(shape), pipeline_mode=pl.Buffered(1))


def _ret_proj(xs, mod, gain, w_bf, colscale, cos, sin, rows):
    tm = rows.tm
    nout = w_bf.shape[1]
    return pl.pallas_call(
        _ret_proj_kernel,
        out_shape=jax.ShapeDtypeStruct((xs.shape[0], nout), BF16),
        grid=(rows.n,),
        in_specs=[
            pl.BlockSpec((tm, D_MODEL), rows.blk()),
            _mod_spec(rows),
            _resident((1, D_MODEL)),
            _resident((D_MODEL, nout)),
            _resident((1, nout)),
            pl.BlockSpec((tm, RET_DK), rows.pos_blk()),
            pl.BlockSpec((tm, RET_DK), rows.pos_blk()),
        ],
        out_specs=pl.BlockSpec((tm, nout), rows.blk()),
        compiler_params=_cparams(("arbitrary",)),
        name="ret_proj",
    )(xs, mod, gain, w_bf, colscale, cos, sin)


def _log_sigmoid(z):
    return jnp.minimum(z, 0.0) - jnp.log1p(jnp.exp(-jnp.abs(z)))


def _gla_proj_kernel(x_ref, mod_ref, gain_ref, w_ref, cs_ref, wa_ref, wbd_ref, bg_ref,
                     o_ref, lg_ref):
    for r in range(x_ref.shape[0] // PROJ_RC):
        rs = slice(r * PROJ_RC, (r + 1) * PROJ_RC)
        h = _norm_mod(x_ref[rs, :], gain_ref[...], mod_ref, 0, 1).astype(BF16)
        a = _dot(h, wa_ref[...]).astype(BF16)
        z = _dot(a, wbd_ref[...]) + bg_ref[...]
        lg_ref[rs, :] = _log_sigmoid(z) * (1.0 / GLA_TAU)
        for j in range(w_ref.shape[1] // PROJ_TN):
            cols = slice(j * PROJ_TN, (j + 1) * PROJ_TN)
            o_ref[rs, cols] = (_dot(h, w_ref[:, cols]) * cs_ref[:, cols]).astype(BF16)


def _gla_proj(xs, mod, gain, w_bf, colscale, wa_bf, wbd_bf, bgate, rows):
    tm = rows.tm
    nout = w_bf.shape[1]
    n = xs.shape[0]
    return pl.pallas_call(
        _gla_proj_kernel,
        out_shape=(jax.ShapeDtypeStruct((n, nout), BF16),
                   jax.ShapeDtypeStruct((n, 2 * GLA_K), F32)),
        grid=(rows.n,),
        in_specs=[
            pl.BlockSpec((tm, D_MODEL), rows.blk()),
            _mod_spec(rows),
            _resident((1, D_MODEL)),
            _resident((D_MODEL, nout)),
            _resident((1, nout)),
            _resident((D_MODEL, 128)),
            _resident((128, 2 * GLA_K)),
            _resident((1, 2 * GLA_K)),
        ],
        out_specs=(pl.BlockSpec((tm, nout), rows.blk()),
                   pl.BlockSpec((tm, 2 * GLA_K), rows.blk())),
        compiler_params=_cparams(("arbitrary",)),
        name="gla_proj",
    )(xs, mod, gain, w_bf, colscale, wa_bf, wbd_bf, bgate)


def _scan_row_maps(batch, seq):
    nlat = seq // SCAN_ROWS
    nctx_blocks = batch * CTX_LEN // SCAN_ROWS

    def fwd(b, c):
        return jnp.where(c == 0, b, nctx_blocks + b * nlat + c - 1)

    def bwd(b, c):
        return jnp.where(c == 0, b, nctx_blocks + b * nlat + nlat - c)
    return fwd, bwd, nlat + 1


def _ret_scan_kernel(ld_ref, qf_ref, kf_ref, vf_ref, qb_ref, kb_ref, vb_ref,
                     of_ref, ob_ref, sf_ref, sb_ref, dm_ref, qd_ref, kd_ref, cd_ref):
    b = pl.program_id(0)
    c = pl.program_id(1)
    n = SCAN_ROWS

    @pl.when(jnp.logical_and(b == 0, c == 0))
    def _():
        ii = lax.broadcasted_iota(jnp.int32, (n, n), 0).astype(F32)
        jj = lax.broadcasted_iota(jnp.int32, (n, n), 1).astype(F32)
        for d in range(2):
            rev = d == 1
            for h in range(RET_HEADS):
                lg = -jnp.exp(ld_ref[d * RET_HEADS + h])[:, :1]
                diff = (jj - ii) if rev else (ii - jj)
                dm_ref[d, h] = jnp.where(diff >= 0, jnp.exp(lg * jnp.maximum(diff, 0.0)), 0.0)
                qd_ref[d, h] = jnp.exp(lg * ((n - ii) if rev else (ii + 1.0)))
                kd_ref[d, h] = jnp.exp(lg * (ii if rev else (n - 1.0 - ii)))
                cd_ref[d, h] = jnp.broadcast_to(jnp.exp(lg * float(n)), (8, 128))

    @pl.when(c == 0)
    def _():
        sf_ref[...] = jnp.zeros_like(sf_ref)
        sb_ref[...] = jnp.zeros_like(sb_ref)

    def one(d, h, q_ref, k_ref, v_ref, o_ref, s_ref):
        qk = slice(h * RET_DK, (h + 1) * RET_DK)
        vv = slice(h * RET_DV, (h + 1) * RET_DV)
        q = q_ref[:, qk]
        k = k_ref[:, qk]
        v = v_ref[:, vv]
        scores = _dot_nt(q, k) * dm_ref[d, h]
        s = s_ref[h]
        o = _dot(scores.astype(BF16), v)
        o = o + _dot((q.astype(F32) * qd_ref[d, h]).astype(BF16), s.astype(BF16))
        o_ref[:, vv] = o
        s_ref[h] = s * cd_ref[d, h][:1, :1] + _dot_tn((k.astype(F32) * kd_ref[d, h]).astype(BF16), v)

    for h in range(RET_HEADS):
        one(0, h, qf_ref, kf_ref, vf_ref, of_ref, sf_ref)
        one(1, h, qb_ref, kb_ref, vb_ref, ob_ref, sb_ref)


def _ret_scan(p, ld, batch, seq):
    fwd, bwd, nsteps = _scan_row_maps(batch, seq)
    n = p.shape[0]
    kcol = (RET_QK + RET_V) // RET_QK
    vcol = (2 * RET_QK + RET_V) // RET_V
    R = SCAN_ROWS

    def specs(rm):
        return [
            pl.BlockSpec((R, RET_QK), lambda b, c: (rm(b, c), 0)),
            pl.BlockSpec((R, RET_QK), lambda b, c: (rm(b, c), kcol)),
            pl.BlockSpec((R, RET_V), lambda b, c: (rm(b, c), vcol)),
        ]
    state = pltpu.VMEM((RET_HEADS, RET_DK, RET_DV), F32)
    table = pltpu.VMEM((2, RET_HEADS, R, R), F32)
    return pl.pallas_call(
        _ret_scan_kernel,
        out_shape=(jax.ShapeDtypeStruct((n, RET_V), F32),
                   jax.ShapeDtypeStruct((n, RET_V), F32)),
        grid=(batch, nsteps),
        in_specs=[pl.BlockSpec((2 * RET_HEADS, 1, 128), lambda b, c: (0, 0, 0))]
        + specs(fwd) + specs(bwd),
        out_specs=(pl.BlockSpec((R, RET_V), lambda b, c: (fwd(b, c), 0)),
                   pl.BlockSpec((R, RET_V), lambda b, c: (bwd(b, c), 0))),
        scratch_shapes=[state, state, table, table, table,
                        pltpu.VMEM((2, RET_HEADS, 8, 128), F32)],
        compiler_params=_cparams(("arbitrary", "arbitrary")),
        name="ret_scan",
    )(ld, p, p, p, p, p, p)


def _gla_tri(rev):
    n = GLA_SUB
    ri = lax.broadcasted_iota(jnp.int32, (n, n), 0)
    ci = lax.broadcasted_iota(jnp.int32, (n, n), 1)
    return (ci >= ri) if rev else (ci <= ri)


def _gla_cum_gates(g_ref, i, rev):
    n = GLA_SUB
    g = g_ref[pl.ds(pl.multiple_of(i * n, n), n), :]
    tmat = jnp.where(_gla_tri(rev), 1.0, 0.0).astype(BF16)
    g_hi = g.astype(BF16)
    g_lo = (g - g_hi.astype(F32)).astype(BF16)
    return _dot(tmat, g_hi) + _dot(tmat, g_lo)


def _gla_sub_chunk(q_ref, k_ref, v_ref, b_all, o_ref, st_ref, h, i, rev):
    n, r = GLA_SUB, GLA_BLK
    nb = n // r
    rows = pl.ds(pl.multiple_of(i * n, n), n)
    kcols = slice(h * GLA_DK, (h + 1) * GLA_DK)
    vcols = slice(h * GLA_DV, (h + 1) * GLA_DV)
    q = q_ref[rows, kcols].astype(F32)
    k = k_ref[rows, kcols].astype(F32)
    v = v_ref[rows, vcols]
    tri = _gla_tri(rev)
    b = b_all[:, kcols]
    b_last = b[0:1] if rev else b[n - 1:n]
    st = st_ref[h]
    o = _dot_nt((q * jnp.exp(b)).astype(BF16), st.astype(BF16))
    k_out = (k * jnp.exp(b_last - b)).astype(BF16)
    st_ref[h] = st * jnp.exp(b_last) + _dot_tn(v, k_out)
    refs = [b[bi * r:bi * r + 1] if rev else b[bi * r + r - 1:bi * r + r] for bi in range(nb)]
    k_hat = jnp.concatenate(
        [k[bi * r:(bi + 1) * r] * jnp.exp(refs[bi] - b[bi * r:(bi + 1) * r]) for bi in range(nb)],
        axis=0).astype(BF16)
    groups = []
    where = {}
    for bj in range(nb):
        for bi in (range(bj, nb) if rev else range(bj + 1)):
            where[(bj, bi)] = len(groups)
            groups.append(q[bj * r:(bj + 1) * r] * jnp.exp(b[bj * r:(bj + 1) * r] - refs[bi]))
    q_cat = jnp.concatenate(groups, axis=0).astype(BF16)
    raw = _dot_nt(q_cat, k_hat)
    col_blk = lax.broadcasted_iota(jnp.int32, (r, n), 1) // r
    row_blocks = []
    for bj in range(nb):
        acc = jnp.zeros((r, n), F32)
        for bi in (range(bj, nb) if rev else range(bj + 1)):
            gi = where[(bj, bi)]
            acc = jnp.where(col_blk == bi, raw[gi * r:(gi + 1) * r], acc)
        row_blocks.append(acc)
    scores = jnp.where(tri, jnp.concatenate(row_blocks, axis=0), 0.0)
    o_ref[rows, vcols] = o + _dot(scores.astype(BF16), v)


def _gla_scan_kernel(qf_ref, kf_ref, vf_ref, gf_ref, qb_ref, kb_ref, vb_ref, gb_ref,
                     of_ref, ob_ref, sf_ref, sb_ref):
    c = pl.program_id(1)
    nsub = SCAN_ROWS // GLA_SUB

    @pl.when(c == 0)
    def _():
        sf_ref[...] = jnp.zeros_like(sf_ref)
        sb_ref[...] = jnp.zeros_like(sb_ref)

    def body(i, carry):
        ib = nsub - 1 - i
        b_f = _gla_cum_gates(gf_ref, i, False)
        b_b = _gla_cum_gates(gb_ref, ib, True)
        for h in range(GLA_HEADS):
            _gla_sub_chunk(qf_ref, kf_ref, vf_ref, b_f, of_ref, sf_ref, h, i, False)
            _gla_sub_chunk(qb_ref, kb_ref, vb_ref, b_b, ob_ref, sb_ref, h, ib, True)
        return carry

    lax.fori_loop(0, nsub, body, 0)


def _gla_scan(p, lg, batch, seq):
    fwd, bwd, nsteps = _scan_row_maps(batch, seq)
    n = p.shape[0]
    kcol = (GLA_K + GLA_V) // GLA_K
    vcol = (2 * GLA_K + GLA_V) // GLA_V
    R = SCAN_ROWS

    def specs(rm, gcol):
        return [
            pl.BlockSpec((R, GLA_K), lambda b, c: (rm(b, c), 0)),
            pl.BlockSpec((R, GLA_K), lambda b, c: (rm(b, c), kcol)),
            pl.BlockSpec((R, GLA_V), lambda b, c: (rm(b, c), vcol)),
            pl.BlockSpec((R, GLA_K), lambda b, c: (rm(b, c), gcol)),
        ]
    state = pltpu.VMEM((GLA_HEADS, GLA_DV, GLA_DK), F32)
    return pl.pallas_call(
        _gla_scan_kernel,
        out_shape=(jax.ShapeDtypeStruct((n, GLA_V), F32),
                   jax.ShapeDtypeStruct((n, GLA_V), F32)),
        grid=(batch, nsteps),
        in_specs=specs(fwd, 0) + specs(bwd, 1),
        out_specs=(pl.BlockSpec((R, GLA_V), lambda b, c: (fwd(b, c), 0)),
                   pl.BlockSpec((R, GLA_V), lambda b, c: (bwd(b, c), 0))),
        scratch_shapes=[state, state],
        compiler_params=_cparams(("arbitrary", "arbitrary")),
        name="gla_scan",
    )(p, p, p, lg, p, p, p, lg)


def _ret_out_kernel(of_ref, ob_ref, g0_ref, g1_ref, x_ref, mod_ref, gnw_ref, gnb_ref, w_ref, o_ref):
    acc = jnp.zeros(o_ref.shape, F32)
    for h in range(RET_HEADS):
        sl = slice(h * RET_DV, (h + 1) * RET_DV)
        o = of_ref[:, sl] + ob_ref[:, sl]
        mu = jnp.mean(o, axis=-1, keepdims=True)
        d = o - mu
        var = jnp.mean(d * d, axis=-1, keepdims=True)
        y = d * lax.rsqrt(var + EPS) * gnw_ref[:, sl] + gnb_ref[:, sl]
        g_ref = g0_ref if h < 2 else g1_ref
        g = g_ref[:, (h % 2) * RET_DV:(h % 2 + 1) * RET_DV].astype(F32)
        acc = acc + _dot((y * _silu(g)).astype(BF16), w_ref[sl, :])
    o_ref[...] = x_ref[...] + mod_ref[2] * acc


def _ret_out(o_f, o_b, p, xs, mod, gn_w, gn_b, w_bf, rows):
    tm = rows.tm
    return pl.pallas_call(
        _ret_out_kernel,
        out_shape=jax.ShapeDtypeStruct(xs.shape, F32),
        grid=(rows.n,),
        in_specs=[
            pl.BlockSpec((tm, RET_V), rows.blk()),
            pl.BlockSpec((tm, RET_V), rows.blk()),
            pl.BlockSpec((tm, 1024), rows.blk(col=1)),
            pl.BlockSpec((tm, 1024), rows.blk(col=2)),
            pl.BlockSpec((tm, D_MODEL), rows.blk()),
            _mod_spec(rows),
            pl.BlockSpec((1, RET_V), lambda t: (0, 0)),
            pl.BlockSpec((1, RET_V), lambda t: (0, 0)),
            pl.BlockSpec((RET_V, D_MODEL), lambda t: (0, 0)),
        ],
        out_specs=pl.BlockSpec((tm, D_MODEL), rows.blk()),
        compiler_params=_cparams(("arbitrary",)),
        name="ret_out",
    )(o_f, o_b, p, p, xs, mod, gn_w, gn_b, w_bf)


def _gla_out_kernel(of_ref, ob_ref, r0_ref, r1_ref, x_ref, mod_ref, ng_ref, w_ref, o_ref):
    acc = jnp.zeros(o_ref.shape, F32)
    for h in range(GLA_HEADS):
        sl = slice(h * GLA_DV, (h + 1) * GLA_DV)
        o = of_ref[:, sl] + ob_ref[:, sl]
        y = o * lax.rsqrt(jnp.mean(o * o, axis=-1, keepdims=True) + EPS) * ng_ref[:, sl]
        r_ref = r0_ref if h < 2 else r1_ref
        r = r_ref[:, (h % 2) * GLA_DV:(h % 2 + 1) * GLA_DV].astype(F32)
        acc = acc + _dot((y * _silu(r)).astype(BF16), w_ref[sl, :])
    o_ref[...] = x_ref[...] + mod_ref[2] * acc


def _gla_out(o_f, o_b, p, xs, mod, norm_g, w_bf, rows, x_base_tile, out_rows):
    tm = rows.tm
    return pl.pallas_call(
        _gla_out_kernel,
        out_shape=jax.ShapeDtypeStruct((out_rows, D_MODEL), F32),
        grid=(rows.n,),
        in_specs=[
            pl.BlockSpec((tm, GLA_V), rows.blk()),
            pl.BlockSpec((tm, GLA_V), rows.blk()),
            pl.BlockSpec((tm, 512), rows.blk(col=1)),
            pl.BlockSpec((tm, 512), rows.blk(col=2)),
            pl.BlockSpec((tm, D_MODEL), rows.blk()),
            _mod_spec(rows),
            pl.BlockSpec((1, GLA_V), lambda t: (0, 0)),
            pl.BlockSpec((GLA_V, D_MODEL), lambda t: (0, 0)),
        ],
        out_specs=pl.BlockSpec((tm, D_MODEL), rows.blk(base_tile=x_base_tile)),
        compiler_params=_cparams(("arbitrary",)),
        name="gla_out",
    )(o_f, o_b, p, p, xs, mod, norm_g, w_bf)


def _ffn_kernel(x_ref, mod_ref, gain_ref, wg_ref, wu_ref, wd_ref, o_ref):
    for r in range(x_ref.shape[0] // PROJ_RC):
        rs = slice(r * PROJ_RC, (r + 1) * PROJ_RC)
        x = x_ref[rs, :]
        h = _norm_mod(x, gain_ref[...], mod_ref, 3, 4).astype(BF16)
        hid = (_silu(_dot(h, wg_ref[...])) * _dot(h, wu_ref[...])).astype(BF16)
        o_ref[rs, :] = x + mod_ref[5] * _dot(hid, wd_ref[...])


def _ffn(xs, mod, gain, wg_bf, wu_bf, wd_bf, rows):
    tm = rows.tm
    return pl.pallas_call(
        _ffn_kernel,
        out_shape=jax.ShapeDtypeStruct(xs.shape, F32),
        grid=(rows.n,),
        in_specs=[
            pl.BlockSpec((tm, D_MODEL), rows.blk()),
            _mod_spec(rows),
            _resident((1, D_MODEL)),
            _resident((D_MODEL, D_FF)),
            _resident((D_MODEL, D_FF)),
            _resident((D_FF, D_MODEL)),
        ],
        out_specs=pl.BlockSpec((tm, D_MODEL), rows.blk()),
        compiler_params=_cparams(("arbitrary",)),
        name="ffn",
    )(xs, mod, gain, wg_bf, wu_bf, wd_bf)


def _router_kernel(x_ref, mod_ref, gain_ref, wr_hi_ref, wr_lo_ref, h_ref, idx_ref, wt_ref):
    h = _norm_mod(x_ref[...], gain_ref[...], mod_ref, 3, 4)
    h_ref[...] = h
    h_hi = h.astype(BF16)
    h_lo = (h - h_hi.astype(F32)).astype(BF16)
    logits = _dot(h_hi, wr_hi_ref[...]) + _dot(h_lo, wr_hi_ref[...]) + _dot(h_hi, wr_lo_ref[...])
    lane = lax.broadcasted_iota(jnp.int32, logits.shape, 1)
    lane_f = lane.astype(F32)
    neg = jnp.float32(-jnp.inf)
    l1 = jnp.where(lane < N_EXPERTS, logits, neg)
    m1 = jnp.max(l1, axis=-1, keepdims=True)
    i1 = jnp.min(jnp.where(l1 == m1, lane_f, 128.0), axis=-1, keepdims=True)
    l2 = jnp.where(lane_f == i1, neg, l1)
    m2 = jnp.max(l2, axis=-1, keepdims=True)
    i2 = jnp.min(jnp.where(l2 == m2, lane_f, 128.0), axis=-1, keepdims=True)
    e2 = jnp.exp(m2 - m1)
    den = 1.0 + e2
    idx_ref[...] = jnp.where(lane == 0, i1, jnp.where(lane == 1, i2, 0.0)).astype(jnp.int32)
    wt_ref[...] = jnp.where(lane == 0, 1.0 / den, jnp.where(lane == 1, e2 / den, 0.0))


def _router(xs, mod, gain, wr_hi, wr_lo, rows, x_base_tile):
    tm = rows.tm
    nrow = rows.n * tm
    out_blk = lambda t: (t, 0)
    return pl.pallas_call(
        _router_kernel,
        out_shape=(jax.ShapeDtypeStruct((nrow, D_MODEL), F32),
                   jax.ShapeDtypeStruct((nrow, 128), jnp.int32),
                   jax.ShapeDtypeStruct((nrow, 128), F32)),
        grid=(rows.n,),
        in_specs=[
            pl.BlockSpec((tm, D_MODEL), rows.blk(base_tile=x_base_tile)),
            _mod_spec(rows),
            pl.BlockSpec((1, D_MODEL), lambda t: (0, 0)),
            pl.BlockSpec((D_MODEL, 128), lambda t: (0, 0)),
            pl.BlockSpec((D_MODEL, 128), lambda t: (0, 0)),
        ],
        out_specs=(pl.BlockSpec((tm, D_MODEL), out_blk),
                   pl.BlockSpec((tm, 128), out_blk),
                   pl.BlockSpec((tm, 128), out_blk)),
        compiler_params=_cparams(("arbitrary",)),
        name="moe_router",
    )(xs, mod, gain, wr_hi, wr_lo)


MOE_TM = 512
MOE_TF = 512


def _expert_kernel(te_ref, ta_ref, src_ref, tok_ref, h_ref, wg_ref, wu_ref, wd_ref, o_ref,
                   hbuf, hb_scr, acc_scr, sem):
    t = pl.program_id(0)
    f = pl.program_id(1)
    active = ta_ref[t] == 1

    nf = D_FF_EXPERT // MOE_TF
    per_step = MOE_TM // nf
    slot = t % 2
    last_sorted = tok_ref.shape[0] - 1

    def start_row(tile, sl, r):
        tok = tok_ref[jnp.minimum(src_ref[tile] + r, last_sorted)]
        pltpu.make_async_copy(h_ref.at[pl.ds(tok, 1)], hbuf.at[sl, pl.ds(r, 1)], sem.at[sl]).start()

    @pl.when(f == 0)
    def _():
        @pl.when(t == 0)
        def _():
            def body(r, carry):
                start_row(0, 0, r)
                return carry
            lax.fori_loop(0, MOE_TM, body, 0, unroll=8)

        requested = jnp.logical_or(t == 0, ta_ref[jnp.maximum(t - 1, 0)] == 1)

        @pl.when(requested)
        def _():
            pltpu.make_async_copy(h_ref.at[pl.ds(0, MOE_TM)], hbuf.at[slot], sem.at[slot]).wait()

        @pl.when(active)
        def _():
            hb_scr[...] = hbuf[slot].astype(BF16)
            for r in range(per_step * nf, MOE_TM):
                start_row(t + 1, 1 - slot, r)

        acc_scr[...] = jnp.zeros_like(acc_scr)

    @pl.when(active)
    def _():
        for r in range(per_step):
            start_row(t + 1, 1 - slot, f * per_step + r)
        h = hb_scr[...]
        hid = (_silu(_dot(h, wg_ref[...])) * _dot(h, wu_ref[...])).astype(BF16)
        acc_scr[...] += _dot(hid, wd_ref[...])

    @pl.when(f == pl.num_programs(1) - 1)
    def _():
        o_ref[...] = acc_scr[...]


def _experts(tile_expert, tile_active, tile_src, sorted_tok, h, wg_bf, wu_bf, wd_bf, layer):
    nslot = tile_expert.shape[0] * MOE_TM
    nf = D_FF_EXPERT // MOE_TF

    def fcol(t, f, ta):
        return jnp.where(ta[t] == 1, f, nf - 1)
    return pl.pallas_call(
        _expert_kernel,
        out_shape=jax.ShapeDtypeStruct((nslot, D_MODEL), F32),
        grid_spec=pltpu.PrefetchScalarGridSpec(
            num_scalar_prefetch=4,
            grid=(nslot // MOE_TM, nf),
            in_specs=[
                pl.BlockSpec(memory_space=pl.ANY),
                pl.BlockSpec((None, None, D_MODEL, MOE_TF),
                             lambda t, f, te, ta, ts, tk: (layer, te[t], 0, fcol(t, f, ta))),
                pl.BlockSpec((None, None, D_MODEL, MOE_TF),
                             lambda t, f, te, ta, ts, tk: (layer, te[t], 0, fcol(t, f, ta))),
                pl.BlockSpec((None, None, MOE_TF, D_MODEL),
                             lambda t, f, te, ta, ts, tk: (layer, te[t], fcol(t, f, ta), 0)),
            ],
            out_specs=pl.BlockSpec((MOE_TM, D_MODEL), lambda t, f, te, ta, ts, tk: (t, 0)),
            scratch_shapes=[pltpu.VMEM((2, MOE_TM, D_MODEL), F32),
                            pltpu.VMEM((MOE_TM, D_MODEL), BF16),
                            pltpu.VMEM((MOE_TM, D_MODEL), F32),
                            pltpu.SemaphoreType.DMA((2,))],
        ),
        compiler_params=_cparams(("arbitrary", "arbitrary")),
        name="moe_experts",
    )(tile_expert, tile_active, tile_src, sorted_tok, h, wg_bf, wu_bf, wd_bf)


COMBINE_TM = 256


def _combine_kernel(pos_ref, ys_ref, x_ref, wt_ref, mod_ref, fg_ref, o_ref, buf, sem, *, final):
    t = pl.program_id(0)
    nt = pl.num_programs(0)
    slot = t % 2

    def start_tile(tile, sl):
        base = tile * COMBINE_TM

        def body(r, carry):
            for k in range(2):
                p = pos_ref[2 * (base + r) + k]
                pltpu.make_async_copy(ys_ref.at[pl.ds(p, 1)], buf.at[sl, k, pl.ds(r, 1)],
                                      sem.at[sl]).start()
            return carry
        lax.fori_loop(0, COMBINE_TM, body, 0, unroll=4)

    @pl.when(t == 0)
    def _():
        start_tile(0, 0)

    @pl.when(t + 1 < nt)
    def _():
        start_tile(t + 1, 1 - slot)

    for k in range(2):
        pltpu.make_async_copy(ys_ref.at[pl.ds(0, COMBINE_TM)], buf.at[slot, k], sem.at[slot]).wait()
    w = wt_ref[...]
    f = w[:, 0:1] * buf[slot, 0] + w[:, 1:2] * buf[slot, 1]
    y = x_ref[...] + mod_ref[5] * f
    if final:
        ms = jnp.mean(y * y, axis=-1, keepdims=True)
        y = y * lax.rsqrt(ms + EPS) * fg_ref[...]
    o_ref[...] = y


def _combine(pos, ys, xs, wts, mod, final_g, rows, x_base_tile, final):
    tm = rows.tm
    nrow = rows.n * tm
    return pl.pallas_call(
        functools.partial(_combine_kernel, final=final),
        out_shape=jax.ShapeDtypeStruct((nrow, D_MODEL), F32),
        grid_spec=pltpu.PrefetchScalarGridSpec(
            num_scalar_prefetch=1,
            grid=(rows.n,),
            in_specs=[
                pl.BlockSpec(memory_space=pl.ANY),
                pl.BlockSpec((tm, D_MODEL), rows.blk(base_tile=x_base_tile)),
                pl.BlockSpec((tm, 128), lambda t, p: (t, 0)),
                _mod_spec(rows),
                pl.BlockSpec((1, D_MODEL), lambda t, p: (0, 0)),
            ],
            out_specs=pl.BlockSpec((tm, D_MODEL), lambda t, p: (t, 0)),
            scratch_shapes=[pltpu.VMEM((2, 2, tm, D_MODEL), F32), pltpu.SemaphoreType.DMA((2,))],
        ),
        compiler_params=_cparams(("arbitrary",)),
        name="moe_combine",
    )(pos, ys, xs, wts, mod, final_g)


def _routing_tables(idx, nslot_pad):
    nrow = idx.shape[0]
    e_flat = idx[:, :2].reshape(-1)
    onehot = (e_flat[:, None] == jnp.arange(N_EXPERTS, dtype=jnp.int32)[None, :]).astype(jnp.int32)
    csum = jnp.cumsum(onehot, axis=0)
    rank = jnp.sum(csum * onehot, axis=1) - 1
    counts = csum[-1]
    padded = ((counts + MOE_TM - 1) // MOE_TM) * MOE_TM
    ends = jnp.cumsum(padded)
    offs = ends - padded
    pos = (jnp.sum(onehot * offs[None, :], axis=1) + rank).astype(jnp.int32)
    nslot = 2 * nrow
    assert nslot <= 1 << 16
    keys = jnp.sort(e_flat * (1 << 16) + jnp.arange(nslot, dtype=jnp.int32))
    sorted_tok = ((keys & 0xFFFF) >> 1).astype(jnp.int32)
    ntile = nslot_pad // MOE_TM
    tile_start = jnp.arange(ntile, dtype=jnp.int32) * MOE_TM
    tile_active = (tile_start < ends[-1]).astype(jnp.int32)
    last_start = jnp.maximum(ends[-1] - MOE_TM, 0)
    tile_start = jnp.minimum(tile_start, last_start)
    tile_expert = jnp.sum((tile_start[:, None] >= ends[None, :]).astype(jnp.int32), axis=1)
    tile_expert = jnp.minimum(tile_expert, N_EXPERTS - 1).astype(jnp.int32)
    te_onehot = (tile_expert[:, None] == jnp.arange(N_EXPERTS, dtype=jnp.int32)[None, :]).astype(jnp.int32)
    group_start = jnp.cumsum(counts) - counts
    tile_src = (jnp.sum(te_onehot * (group_start - offs)[None, :], axis=1) + tile_start).astype(jnp.int32)
    return pos, sorted_tok, tile_src, tile_expert, tile_active


def _moe(xs, mod, gain, wr, wg_bf, wu_bf, wd_bf, layer, final_g, rows, x_base_tile, final):
    wr_pad = jnp.zeros((D_MODEL, 128), F32).at[:, :N_EXPERTS].set(wr)
    wr_hi = wr_pad.astype(BF16)
    wr_lo = (wr_pad - wr_hi.astype(F32)).astype(BF16)
    h, idx, wts = _router(xs, mod, gain, wr_hi, wr_lo, rows, x_base_tile)
    nrow = h.shape[0]
    nslot_pad = 2 * nrow + N_EXPERTS * MOE_TM
    pos, sorted_tok, tile_src, tile_expert, tile_active = _routing_tables(idx, nslot_pad)
    ys = _experts(tile_expert, tile_active, tile_src, sorted_tok, h, wg_bf, wu_bf, wd_bf, layer)
    return _combine(pos, ys, xs, wts, mod, final_g, rows, x_base_tile, final)


def kernel(x, c, ctx, c_ctx, ada_w, ada_b, norm_mix_g, norm_ffn_g, final_g, ret_w_in, ret_log_decay,
           ret_gn_w, ret_gn_b, ret_w_out, gla_w_in, gla_w_gate_up, gla_b_gate, gla_norm_g, gla_w_out,
           ffn_w_gate, ffn_w_up, ffn_w_down, moe_w_router, moe_w_gate, moe_w_up, moe_w_down):
    batch, seq, d = x.shape
    assert d == D_MODEL and ctx.shape == (batch, CTX_LEN, d) and seq % 1024 == 0
    nctx = batch * CTX_LEN
    xs = jnp.concatenate([ctx.reshape(nctx, d), x.reshape(batch * seq, d)], axis=0)
    mod_all = _modulation(c, c_ctx, ada_w, ada_b)

    proj_tm = 512
    cos, sin = _rope_tables(seq, proj_tm)
    ret_colscale = jnp.concatenate([
        jnp.ones((1, RET_QK + RET_V), F32),
        jnp.full((1, RET_QK), RET_DK ** -0.5, F32),
        jnp.ones((1, RET_V), F32)], axis=1)
    gla_main = 2 * GLA_K + 2 * GLA_V
    gla_colscale = jnp.concatenate([
        jnp.full((1, GLA_K), GLA_DK ** -0.5, F32),
        jnp.ones((1, gla_main - GLA_K), F32)], axis=1)

    final_row = final_g.reshape(1, d)
    moe_gate_bf = moe_w_gate.astype(BF16)
    moe_up_bf = moe_w_up.astype(BF16)
    moe_down_bf = moe_w_down.astype(BF16)
    for i in range(DEPTH):
        last = i == DEPTH - 1
        j = i // 2
        mod = mod_all[i]
        rows_all_proj = _Rows(batch, seq, proj_tm, False)
        rows_all = _Rows(batch, seq, 512, False)
        rows_lat = _Rows(batch, seq, 512, True)
        mix_gain = norm_mix_g[i].reshape(1, d)
        ffn_gain = norm_ffn_g[i].reshape(1, d)
        if i % 2 == 0:
            p = _ret_proj(xs, mod, mix_gain, ret_w_in[j].astype(BF16), ret_colscale, cos, sin,
                          rows_all_proj)
            ld = jnp.broadcast_to(ret_log_decay[j].reshape(2 * RET_HEADS, 1, 1), (2 * RET_HEADS, 1, 128))
            o_f, o_b = _ret_scan(p, ld, batch, seq)
            xs = _ret_out(o_f, o_b, p, xs, mod, ret_gn_w[j].reshape(1, RET_V),
                          ret_gn_b[j].reshape(1, RET_V), ret_w_out[j].astype(BF16), rows_all)
            xs = _ffn(xs, mod, ffn_gain, ffn_w_gate[j].astype(BF16), ffn_w_up[j].astype(BF16),
                      ffn_w_down[j].astype(BF16), rows_all)
        else:
            w_in = gla_w_in[j]
            wa = jnp.zeros((d, 128), F32).at[:, :2 * GLA_GATE_RANK].set(w_in[:, gla_main:])
            wbd = jnp.zeros((128, 2 * GLA_K), F32)
            wbd = wbd.at[:GLA_GATE_RANK, :GLA_K].set(gla_w_gate_up[j, 0])
            wbd = wbd.at[GLA_GATE_RANK:2 * GLA_GATE_RANK, GLA_K:].set(gla_w_gate_up[j, 1])
            p, lg = _gla_proj(xs, mod, mix_gain, w_in[:, :gla_main].astype(BF16), gla_colscale,
                              wa.astype(BF16), wbd.astype(BF16), gla_b_gate[j].reshape(1, 2 * GLA_K),
                              rows_all_proj)
            o_f, o_b = _gla_scan(p, lg, batch, seq)
            rows = rows_lat if last else rows_all
            base = rows.start if last else 0
            xs = _gla_out(o_f, o_b, p, xs, mod, gla_norm_g[j].reshape(1, GLA_V),
                          gla_w_out[j].astype(BF16), rows, base, rows.n * rows.tm)
            rows_c = _Rows(batch, seq, COMBINE_TM, last)
            base_c = rows_c.start if last else 0
            xs = _moe(xs, mod, ffn_gain, moe_w_router[j], moe_gate_bf, moe_up_bf, moe_down_bf, j,
                      final_row, rows_c, base_c, last)
    return xs.reshape(batch, seq, d)
```

```python
import functools

import jax
import jax.numpy as jnp
from jax import lax
from jax.experimental import pallas as pl
from jax.experimental.pallas import tpu as pltpu

D_MODEL = 1024
GRID_W = 64
CTX_LEN = 256
DEPTH = 4
RET_HEADS = 4
RET_DK = 256
RET_DV = 512
RET_QK = 1024
RET_V = 2048
GLA_HEADS = 4
GLA_DK = 128
GLA_DV = 256
GLA_K = 512
GLA_V = 1024
GLA_GATE_RANK = 16
GLA_TAU = 16.0
D_FF = 2816
N_EXPERTS = 8
D_FF_EXPERT = 3584
ROPE_BASE = 10000.0
EPS = 1e-6

BF16 = jnp.bfloat16
F32 = jnp.float32

SCAN_ROWS = 256
GLA_SUB = 128
GLA_BLK = 16
VMEM_LIMIT = 48 * 1024 * 1024


def _cparams(sem):
    return pltpu.CompilerParams(dimension_semantics=sem, vmem_limit_bytes=VMEM_LIMIT)


def _dot(a, b):
    return jnp.dot(a, b, preferred_element_type=F32)


def _dot_nt(a, b):
    return lax.dot_general(a, b, (((1,), (1,)), ((), ())), preferred_element_type=F32)


def _dot_tn(a, b):
    return lax.dot_general(a, b, (((0,), (0,)), ((), ())), preferred_element_type=F32)


def _silu(x):
    return x * jax.nn.sigmoid(x)


def _norm_mod(x, gain, mod_ref, shift_i, scale_i):
    ms = jnp.mean(x * x, axis=-1, keepdims=True)
    y = x * lax.rsqrt(ms + EPS) * gain
    return y * (1.0 + mod_ref[scale_i]) + mod_ref[shift_i]


class _Rows:
    def __init__(self, batch, seq, tm, lat_only):
        self.tm = tm
        self.nct = batch * CTX_LEN // tm
        self.tpb = seq // tm
        self.start = self.nct if lat_only else 0
        self.n = batch * seq // tm + (0 if lat_only else self.nct)
        self.batch = batch

    def blk(self, base_tile=0, col=0):
        off = self.start - base_tile
        return lambda t, *_: (t + off, col)

    def mod_blk(self):
        start, nct, tpb, batch = self.start, self.nct, self.tpb, self.batch

        def im(t, *_):
            tg = t + start
            return (jnp.where(tg < nct, batch, (tg - nct) // tpb), 0, 0, 0)
        return im

    def pos_blk(self):
        start, nct, tpb = self.start, self.nct, self.tpb

        def im(t, *_):
            tg = t + start
            return (jnp.where(tg < nct, 0, 1 + (tg - nct) % tpb), 0)
        return im


def _mod_spec(rows):
    return pl.BlockSpec((None, 6, 1, D_MODEL), rows.mod_blk())


def _mod_kernel(c_ref, w_ref, b_ref, o_ref):
    s = _silu(c_ref[...]).astype(BF16)
    o_ref[...] = _dot(s, w_ref[...].astype(BF16)) + b_ref[...]


def _modulation(c, c_ctx, ada_w, ada_b):
    batch = c.shape[0]
    nrow = 16
    cc = jnp.concatenate([c, c_ctx[None], jnp.zeros((nrow - batch - 1, D_MODEL), F32)], axis=0)
    out = pl.pallas_call(
        _mod_kernel,
        out_shape=jax.ShapeDtypeStruct((DEPTH, 6, nrow, D_MODEL), F32),
        grid=(DEPTH, 6),
        in_specs=[
            pl.BlockSpec((nrow, D_MODEL), lambda i, k: (0, 0)),
            pl.BlockSpec((None, D_MODEL, D_MODEL), lambda i, k: (i, 0, k)),
            pl.BlockSpec((None, None, 1, D_MODEL), lambda i, k: (i, k, 0, 0)),
        ],
        out_specs=pl.BlockSpec((None, None, nrow, D_MODEL), lambda i, k: (i, k, 0, 0)),
        compiler_params=_cparams(("arbitrary", "arbitrary")),
        name="adaln_mod",
    )(cc, ada_w, ada_b.reshape(DEPTH, 6, 1, D_MODEL))
    return out.transpose(0, 2, 1, 3)[:, :, :, None, :]


PROJ_TN = 1024
PROJ_RC = 256


def _ret_proj_kernel(x_ref, mod_ref, gain_ref, w_ref, cs_ref, cos_ref, sin_ref, o_ref):
    rope_tiles = (0, (RET_QK + RET_V) // PROJ_TN)
    for r in range(x_ref.shape[0] // PROJ_RC):
        rs = slice(r * PROJ_RC, (r + 1) * PROJ_RC)
        h = _norm_mod(x_ref[rs, :], gain_ref[...], mod_ref, 0, 1).astype(BF16)
        for j in range(w_ref.shape[1] // PROJ_TN):
            cols = slice(j * PROJ_TN, (j + 1) * PROJ_TN)
            acc = _dot(h, w_ref[:, cols]) * cs_ref[:, cols]
            if j not in rope_tiles:
                o_ref[rs, cols] = acc.astype(BF16)
                continue
            for blk in range(PROJ_TN // 128):
                lo = blk * 128
                tl = (blk % 2) * 128
                a = acc[:, lo:lo + 128]
                rot = pltpu.roll(a, 64, 1)
                o_ref[rs, j * PROJ_TN + lo:j * PROJ_TN + lo + 128] = (
                    a * cos_ref[rs, tl:tl + 128] + rot * sin_ref[rs, tl:tl + 128]).astype(BF16)


def _rope_tables(seq, tm):
    half = RET_DK // 4
    freqs = ROPE_BASE ** (-jnp.arange(half, dtype=F32) / half)
    rows = seq // GRID_W
    row = jnp.broadcast_to(jnp.arange(rows, dtype=F32)[:, None], (rows, GRID_W)).reshape(seq)
    col = jnp.broadcast_to(jnp.arange(GRID_W, dtype=F32)[None, :], (rows, GRID_W)).reshape(seq)
    ar = row[:, None] * freqs[None, :]
    ac = col[:, None] * freqs[None, :]
    cos = jnp.concatenate([jnp.cos(ar), jnp.cos(ar), jnp.cos(ac), jnp.cos(ac)], axis=-1)
    sin = jnp.concatenate([-jnp.sin(ar), jnp.sin(ar), -jnp.sin(ac), jnp.sin(ac)], axis=-1)
    cos = jnp.concatenate([jnp.ones((tm, RET_DK), F32), cos], axis=0)
    sin = jnp.concatenate([jnp.zeros((tm, RET_DK), F32), sin], axis=0)
    return cos, sin


def _resident(shape):
    return pl.BlockSpec(shape, lambda *_: (0,) * len(shape), pipeline_mode=pl.Buffered(1))


def _ret_proj(xs, mod, gain, w_bf, colscale, cos, sin, rows):
    tm = rows.tm
    nout = w_bf.shape[1]
    return pl.pallas_call(
        _ret_proj_kernel,
        out_shape=jax.ShapeDtypeStruct((xs.shape[0], nout), BF16),
        grid=(rows.n,),
        in_specs=[
            pl.BlockSpec((tm, D_MODEL), rows.blk()),
            _mod_spec(rows),
            _resident((1, D_MODEL)),
            _resident((D_MODEL, nout)),
            _resident((1, nout)),
            pl.BlockSpec((tm, RET_DK), rows.pos_blk()),
            pl.BlockSpec((tm, RET_DK), rows.pos_blk()),
        ],
        out_specs=pl.BlockSpec((tm, nout), rows.blk()),
        compiler_params=_cparams(("arbitrary",)),
        name="ret_proj",
    )(xs, mod, gain, w_bf, colscale, cos, sin)


def _log_sigmoid(z):
    return jnp.minimum(z, 0.0) - jnp.log1p(jnp.exp(-jnp.abs(z)))


def _gla_proj_kernel(x_ref, mod_ref, gain_ref, w_ref, cs_ref, wa_ref, wbd_ref, bg_ref,
                     o_ref, lg_ref):
    for r in range(x_ref.shape[0] // PROJ_RC):
        rs = slice(r * PROJ_RC, (r + 1) * PROJ_RC)
        h = _norm_mod(x_ref[rs, :], gain_ref[...], mod_ref, 0, 1).astype(BF16)
        a = _dot(h, wa_ref[...]).astype(BF16)
        z = _dot(a, wbd_ref[...]) + bg_ref[...]
        lg_ref[rs, :] = _log_sigmoid(z) * (1.0 / GLA_TAU)
        for j in range(w_ref.shape[1] // PROJ_TN):
            cols = slice(j * PROJ_TN, (j + 1) * PROJ_TN)
            o_ref[rs, cols] = (_dot(h, w_ref[:, cols]) * cs_ref[:, cols]).astype(BF16)


def _gla_proj(xs, mod, gain, w_bf, colscale, wa_bf, wbd_bf, bgate, rows):
    tm = rows.tm
    nout = w_bf.shape[1]
    n = xs.shape[0]
    return pl.pallas_call(
        _gla_proj_kernel,
        out_shape=(jax.ShapeDtypeStruct((n, nout), BF16),
                   jax.ShapeDtypeStruct((n, 2 * GLA_K), F32)),
        grid=(rows.n,),
        in_specs=[
            pl.BlockSpec((tm, D_MODEL), rows.blk()),
            _mod_spec(rows),
            _resident((1, D_MODEL)),
            _resident((D_MODEL, nout)),
            _resident((1, nout)),
            _resident((D_MODEL, 128)),
            _resident((128, 2 * GLA_K)),
            _resident((1, 2 * GLA_K)),
        ],
        out_specs=(pl.BlockSpec((tm, nout), rows.blk()),
                   pl.BlockSpec((tm, 2 * GLA_K), rows.blk())),
        compiler_params=_cparams(("arbitrary",)),
        name="gla_proj",
    )(xs, mod, gain, w_bf, colscale, wa_bf, wbd_bf, bgate)


def _scan_row_maps(batch, seq):
    nlat = seq // SCAN_ROWS
    nctx_blocks = batch * CTX_LEN // SCAN_ROWS

    def fwd(b, c):
        return jnp.where(c == 0, b, nctx_blocks + b * nlat + c - 1)

    def bwd(b, c):
        return jnp.where(c == 0, b, nctx_blocks + b * nlat + nlat - c)
    return fwd, bwd, nlat + 1


def _ret_scan_kernel(ld_ref, qf_ref, kf_ref, vf_ref, qb_ref, kb_ref, vb_ref,
                     of_ref, ob_ref, sf_ref, sb_ref, dm_ref, qd_ref, kd_ref, cd_ref):
    b = pl.program_id(0)
    c = pl.program_id(1)
    n = SCAN_ROWS

    @pl.when(jnp.logical_and(b == 0, c == 0))
    def _():
        ii = lax.broadcasted_iota(jnp.int32, (n, n), 0).astype(F32)
        jj = lax.broadcasted_iota(jnp.int32, (n, n), 1).astype(F32)
        for d in range(2):
            rev = d == 1
            for h in range(RET_HEADS):
                lg = -jnp.exp(ld_ref[d * RET_HEADS + h])[:, :1]
                diff = (jj - ii) if rev else (ii - jj)
                dm_ref[d, h] = jnp.where(diff >= 0, jnp.exp(lg * jnp.maximum(diff, 0.0)), 0.0)
                qd_ref[d, h] = jnp.exp(lg * ((n - ii) if rev else (ii + 1.0)))
                kd_ref[d, h] = jnp.exp(lg * (ii if rev else (n - 1.0 - ii)))
                cd_ref[d, h] = jnp.broadcast_to(jnp.exp(lg * float(n)), (8, 128))

    @pl.when(c == 0)
    def _():
        sf_ref[...] = jnp.zeros_like(sf_ref)
        sb_ref[...] = jnp.zeros_like(sb_ref)

    def one(d, h, q_ref, k_ref, v_ref, o_ref, s_ref):
        qk = slice(h * RET_DK, (h + 1) * RET_DK)
        vv = slice(h * RET_DV, (h + 1) * RET_DV)
        q = q_ref[:, qk]
        k = k_ref[:, qk]
        v = v_ref[:, vv]
        scores = _dot_nt(q, k) * dm_ref[d, h]
        s = s_ref[h]
        o = _dot(scores.astype(BF16), v)
        o = o + _dot((q.astype(F32) * qd_ref[d, h]).astype(BF16), s.astype(BF16))
        o_ref[:, vv] = o.astype(o_ref.dtype)
        s_ref[h] = s * cd_ref[d, h][:1, :1] + _dot_tn((k.astype(F32) * kd_ref[d, h]).astype(BF16), v)

    for h in range(RET_HEADS):
        one(0, h, qf_ref, kf_ref, vf_ref, of_ref, sf_ref)
        one(1, h, qb_ref, kb_ref, vb_ref, ob_ref, sb_ref)


def _ret_scan(p, ld, batch, seq):
    fwd, bwd, nsteps = _scan_row_maps(batch, seq)
    n = p.shape[0]
    kcol = (RET_QK + RET_V) // RET_QK
    vcol = (2 * RET_QK + RET_V) // RET_V
    R = SCAN_ROWS

    def specs(rm):
        return [
            pl.BlockSpec((R, RET_QK), lambda b, c: (rm(b, c), 0)),
            pl.BlockSpec((R, RET_QK), lambda b, c: (rm(b, c), kcol)),
            pl.BlockSpec((R, RET_V), lambda b, c: (rm(b, c), vcol)),
        ]
    state = pltpu.VMEM((RET_HEADS, RET_DK, RET_DV), F32)
    table = pltpu.VMEM((2, RET_HEADS, R, R), F32)
    return pl.pallas_call(
        _ret_scan_kernel,
        out_shape=(jax.ShapeDtypeStruct((n, RET_V), BF16),
                   jax.ShapeDtypeStruct((n, RET_V), BF16)),
        grid=(batch, nsteps),
        in_specs=[pl.BlockSpec((2 * RET_HEADS, 1, 128), lambda b, c: (0, 0, 0))]
        + specs(fwd) + specs(bwd),
        out_specs=(pl.BlockSpec((R, RET_V), lambda b, c: (fwd(b, c), 0)),
                   pl.BlockSpec((R, RET_V), lambda b, c: (bwd(b, c), 0))),
        scratch_shapes=[state, state, table, table, table,
                        pltpu.VMEM((2, RET_HEADS, 8, 128), F32)],
        compiler_params=_cparams(("arbitrary", "arbitrary")),
        name="ret_scan",
    )(ld, p, p, p, p, p, p)


def _gla_tri(rev):
    n = GLA_SUB
    ri = lax.broadcasted_iota(jnp.int32, (n, n), 0)
    ci = lax.broadcasted_iota(jnp.int32, (n, n), 1)
    return (ci >= ri) if rev else (ci <= ri)


def _gla_cum_gates(g_ref, i, rev):
    n = GLA_SUB
    g = g_ref[i * n:(i + 1) * n, :]
    tmat = jnp.where(_gla_tri(rev), 1.0, 0.0).astype(BF16)
    g_hi = g.astype(BF16)
    g_lo = (g - g_hi.astype(F32)).astype(BF16)
    return _dot(tmat, g_hi) + _dot(tmat, g_lo)


def _gla_pairs(rev):
    nb = GLA_SUB // GLA_BLK
    return [(bj, bi) for bj in range(nb) for bi in (range(bj, nb) if rev else range(bj + 1))]


def _gla_operands(q_ref, k_ref, v_ref, b_all, h, i, rev):
    n, r = GLA_SUB, GLA_BLK
    nb = n // r
    rows = slice(i * n, (i + 1) * n)
    kcols = slice(h * GLA_DK, (h + 1) * GLA_DK)
    q = q_ref[rows, kcols].astype(F32)
    k = k_ref[rows, kcols].astype(F32)
    v = v_ref[rows, h * GLA_DV:(h + 1) * GLA_DV]
    b = b_all[:, kcols]
    b_last = b[0:1] if rev else b[n - 1:n]
    q_in = (q * jnp.exp(b)).astype(BF16)
    k_out = (k * jnp.exp(b_last - b)).astype(BF16)
    refs = [b[bi * r:bi * r + 1] if rev else b[bi * r + r - 1:bi * r + r] for bi in range(nb)]
    k_hat = jnp.concatenate(
        [k[bi * r:(bi + 1) * r] * jnp.exp(refs[bi] - b[bi * r:(bi + 1) * r]) for bi in range(nb)],
        axis=0).astype(BF16)
    q_cat = jnp.concatenate(
        [q[bj * r:(bj + 1) * r] * jnp.exp(b[bj * r:(bj + 1) * r] - refs[bi])
         for bj, bi in _gla_pairs(rev)], axis=0).astype(BF16)
    return dict(v=v, q_in=q_in, k_out=k_out, k_hat=k_hat, q_cat=q_cat, decay=jnp.exp(b_last))


def _gla_scores(raw, rev):
    n, r = GLA_SUB, GLA_BLK
    pairs = _gla_pairs(rev)
    col_blk = lax.broadcasted_iota(jnp.int32, (r, n), 1) // r
    row_blocks = []
    for bj in range(n // r):
        acc = jnp.zeros((r, n), F32)
        for gi, (pj, bi) in enumerate(pairs):
            if pj == bj:
                acc = jnp.where(col_blk == bi, raw[gi * r:(gi + 1) * r], acc)
        row_blocks.append(acc)
    return jnp.where(_gla_tri(rev), jnp.concatenate(row_blocks, axis=0), 0.0).astype(BF16)


def _gla_chunk_group(chains):
    ops = [_gla_operands(qr, kr, vr, b_all, h, i, rev) for (qr, kr, vr, b_all, _, h, i, rev) in chains]
    raws = [_dot_nt(o["q_cat"], o["k_hat"]) for o in ops]
    inter = [_dot_nt(o["q_in"], c[4].astype(BF16)) for o, c in zip(ops, chains)]
    kv = [_dot_tn(o["v"], o["k_out"]) for o in ops]
    scores = [_gla_scores(raw, c[7]) for raw, c in zip(raws, chains)]
    outs = [it + _dot(s, o["v"]) for it, s, o in zip(inter, scores, ops)]
    states = [c[4] * o["decay"] + x for c, o, x in zip(chains, ops, kv)]
    return outs, states


def _gla_scan_kernel(qf_ref, kf_ref, vf_ref, gf_ref, qb_ref, kb_ref, vb_ref, gb_ref,
                     of_ref, ob_ref, sf_ref, sb_ref):
    c = pl.program_id(1)
    nsub = SCAN_ROWS // GLA_SUB

    @pl.when(c == 0)
    def _():
        sf_ref[...] = jnp.zeros_like(sf_ref)
        sb_ref[...] = jnp.zeros_like(sb_ref)

    st_f = [sf_ref[h] for h in range(GLA_HEADS)]
    st_b = [sb_ref[h] for h in range(GLA_HEADS)]
    n = GLA_SUB
    for i in range(nsub):
        ib = nsub - 1 - i
        b_f = _gla_cum_gates(gf_ref, i, False)
        b_b = _gla_cum_gates(gb_ref, ib, True)
        chains = []
        for h in range(GLA_HEADS):
            chains.append((qf_ref, kf_ref, vf_ref, b_f, st_f[h], h, i, False))
            chains.append((qb_ref, kb_ref, vb_ref, b_b, st_b[h], h, ib, True))
        outs, states = _gla_chunk_group(chains)
        for h in range(GLA_HEADS):
            vcols = slice(h * GLA_DV, (h + 1) * GLA_DV)
            of_ref[i * n:(i + 1) * n, vcols] = outs[2 * h].astype(of_ref.dtype)
            ob_ref[ib * n:(ib + 1) * n, vcols] = outs[2 * h + 1].astype(ob_ref.dtype)
            st_f[h] = states[2 * h]
            st_b[h] = states[2 * h + 1]
    for h in range(GLA_HEADS):
        sf_ref[h] = st_f[h]
        sb_ref[h] = st_b[h]


def _gla_scan(p, lg, batch, seq):
    fwd, bwd, nsteps = _scan_row_maps(batch, seq)
    n = p.shape[0]
    kcol = (GLA_K + GLA_V) // GLA_K
    vcol = (2 * GLA_K + GLA_V) // GLA_V
    R = SCAN_ROWS

    def specs(rm, gcol):
        return [
            pl.BlockSpec((R, GLA_K), lambda b, c: (rm(b, c), 0)),
            pl.BlockSpec((R, GLA_K), lambda b, c: (rm(b, c), kcol)),
            pl.BlockSpec((R, GLA_V), lambda b, c: (rm(b, c), vcol)),
            pl.BlockSpec((R, GLA_K), lambda b, c: (rm(b, c), gcol)),
        ]
    state = pltpu.VMEM((GLA_HEADS, GLA_DV, GLA_DK), F32)
    return pl.pallas_call(
        _gla_scan_kernel,
        out_shape=(jax.ShapeDtypeStruct((n, GLA_V), BF16),
                   jax.ShapeDtypeStruct((n, GLA_V), BF16)),
        grid=(batch, nsteps),
        in_specs=specs(fwd, 0) + specs(bwd, 1),
        out_specs=(pl.BlockSpec((R, GLA_V), lambda b, c: (fwd(b, c), 0)),
                   pl.BlockSpec((R, GLA_V), lambda b, c: (bwd(b, c), 0))),
        scratch_shapes=[state, state],
        compiler_params=_cparams(("arbitrary", "arbitrary")),
        name="gla_scan",
    )(p, p, p, lg, p, p, p, lg)


def _ret_out_kernel(of_ref, ob_ref, g0_ref, g1_ref, x_ref, mod_ref, gnw_ref, gnb_ref, w_ref, o_ref):
    acc = jnp.zeros(o_ref.shape, F32)
    for h in range(RET_HEADS):
        sl = slice(h * RET_DV, (h + 1) * RET_DV)
        o = of_ref[:, sl].astype(F32) + ob_ref[:, sl].astype(F32)
        mu = jnp.mean(o, axis=-1, keepdims=True)
        d = o - mu
        var = jnp.mean(d * d, axis=-1, keepdims=True)
        y = d * lax.rsqrt(var + EPS) * gnw_ref[:, sl] + gnb_ref[:, sl]
        g_ref = g0_ref if h < 2 else g1_ref
        g = g_ref[:, (h % 2) * RET_DV:(h % 2 + 1) * RET_DV].astype(F32)
        acc = acc + _dot((y * _silu(g)).astype(BF16), w_ref[sl, :])
    o_ref[...] = x_ref[...] + mod_ref[2] * acc


def _ret_out(o_f, o_b, p, xs, mod, gn_w, gn_b, w_bf, rows):
    tm = rows.tm
    return pl.pallas_call(
        _ret_out_kernel,
        out_shape=jax.ShapeDtypeStruct(xs.shape, F32),
        grid=(rows.n,),
        in_specs=[
            pl.BlockSpec((tm, RET_V), rows.blk()),
            pl.BlockSpec((tm, RET_V), rows.blk()),
            pl.BlockSpec((tm, 1024), rows.blk(col=1)),
            pl.BlockSpec((tm, 1024), rows.blk(col=2)),
            pl.BlockSpec((tm, D_MODEL), rows.blk()),
            _mod_spec(rows),
            pl.BlockSpec((1, RET_V), lambda t: (0, 0)),
            pl.BlockSpec((1, RET_V), lambda t: (0, 0)),
            pl.BlockSpec((RET_V, D_MODEL), lambda t: (0, 0)),
        ],
        out_specs=pl.BlockSpec((tm, D_MODEL), rows.blk()),
        compiler_params=_cparams(("arbitrary",)),
        name="ret_out",
    )(o_f, o_b, p, p, xs, mod, gn_w, gn_b, w_bf)


def _gla_out_kernel(of_ref, ob_ref, r0_ref, r1_ref, x_ref, mod_ref, ng_ref, w_ref, o_ref):
    acc = jnp.zeros(o_ref.shape, F32)
    for h in range(GLA_HEADS):
        sl = slice(h * GLA_DV, (h + 1) * GLA_DV)
        o = of_ref[:, sl].astype(F32) + ob_ref[:, sl].astype(F32)
        y = o * lax.rsqrt(jnp.mean(o * o, axis=-1, keepdims=True) + EPS) * ng_ref[:, sl]
        r_ref = r0_ref if h < 2 else r1_ref
        r = r_ref[:, (h % 2) * GLA_DV:(h % 2 + 1) * GLA_DV].astype(F32)
        acc = acc + _dot((y * _silu(r)).astype(BF16), w_ref[sl, :])
    o_ref[...] = x_ref[...] + mod_ref[2] * acc


def _gla_out(o_f, o_b, p, xs, mod, norm_g, w_bf, rows, x_base_tile, out_rows):
    tm = rows.tm
    return pl.pallas_call(
        _gla_out_kernel,
        out_shape=jax.ShapeDtypeStruct((out_rows, D_MODEL), F32),
        grid=(rows.n,),
        in_specs=[
            pl.BlockSpec((tm, GLA_V), rows.blk()),
            pl.BlockSpec((tm, GLA_V), rows.blk()),
            pl.BlockSpec((tm, 512), rows.blk(col=1)),
            pl.BlockSpec((tm, 512), rows.blk(col=2)),
            pl.BlockSpec((tm, D_MODEL), rows.blk()),
            _mod_spec(rows),
            pl.BlockSpec((1, GLA_V), lambda t: (0, 0)),
            pl.BlockSpec((GLA_V, D_MODEL), lambda t: (0, 0)),
        ],
        out_specs=pl.BlockSpec((tm, D_MODEL), rows.blk(base_tile=x_base_tile)),
        compiler_params=_cparams(("arbitrary",)),
        name="gla_out",
    )(o_f, o_b, p, p, xs, mod, norm_g, w_bf)


def _ffn_kernel(x_ref, mod_ref, gain_ref, wg_ref, wu_ref, wd_ref, o_ref):
    for r in range(x_ref.shape[0] // PROJ_RC):
        rs = slice(r * PROJ_RC, (r + 1) * PROJ_RC)
        x = x_ref[rs, :]
        h = _norm_mod(x, gain_ref[...], mod_ref, 3, 4).astype(BF16)
        hid = (_silu(_dot(h, wg_ref[...])) * _dot(h, wu_ref[...])).astype(BF16)
        o_ref[rs, :] = x + mod_ref[5] * _dot(hid, wd_ref[...])


def _ffn(xs, mod, gain, wg_bf, wu_bf, wd_bf, rows):
    tm = rows.tm
    return pl.pallas_call(
        _ffn_kernel,
        out_shape=jax.ShapeDtypeStruct(xs.shape, F32),
        grid=(rows.n,),
        in_specs=[
            pl.BlockSpec((tm, D_MODEL), rows.blk()),
            _mod_spec(rows),
            _resident((1, D_MODEL)),
            _resident((D_MODEL, D_FF)),
            _resident((D_MODEL, D_FF)),
            _resident((D_FF, D_MODEL)),
        ],
        out_specs=pl.BlockSpec((tm, D_MODEL), rows.blk()),
        compiler_params=_cparams(("arbitrary",)),
        name="ffn",
    )(xs, mod, gain, wg_bf, wu_bf, wd_bf)


def _router_kernel(x_ref, mod_ref, gain_ref, wr_hi_ref, wr_lo_ref, h_ref, idx_ref, wt_ref):
    h = _norm_mod(x_ref[...], gain_ref[...], mod_ref, 3, 4)
    h_ref[...] = h
    h_hi = h.astype(BF16)
    h_lo = (h - h_hi.astype(F32)).astype(BF16)
    logits = _dot(h_hi, wr_hi_ref[...]) + _dot(h_lo, wr_hi_ref[...]) + _dot(h_hi, wr_lo_ref[...])
    lane = lax.broadcasted_iota(jnp.int32, logits.shape, 1)
    lane_f = lane.astype(F32)
    neg = jnp.float32(-jnp.inf)
    l1 = jnp.where(lane < N_EXPERTS, logits, neg)
    m1 = jnp.max(l1, axis=-1, keepdims=True)
    i1 = jnp.min(jnp.where(l1 == m1, lane_f, 128.0), axis=-1, keepdims=True)
    l2 = jnp.where(lane_f == i1, neg, l1)
    m2 = jnp.max(l2, axis=-1, keepdims=True)
    i2 = jnp.min(jnp.where(l2 == m2, lane_f, 128.0), axis=-1, keepdims=True)
    e2 = jnp.exp(m2 - m1)
    den = 1.0 + e2
    idx_ref[...] = jnp.where(lane == 0, i1, jnp.where(lane == 1, i2, 0.0)).astype(jnp.int32)
    wt_ref[...] = jnp.where(lane == 0, 1.0 / den, jnp.where(lane == 1, e2 / den, 0.0))


def _router(xs, mod, gain, wr_hi, wr_lo, rows, x_base_tile):
    tm = rows.tm
    nrow = rows.n * tm
    out_blk = lambda t: (t, 0)
    return pl.pallas_call(
        _router_kernel,
        out_shape=(jax.ShapeDtypeStruct((nrow, D_MODEL), F32),
                   jax.ShapeDtypeStruct((nrow, 128), jnp.int32),
                   jax.ShapeDtypeStruct((nrow, 128), F32)),
        grid=(rows.n,),
        in_specs=[
            pl.BlockSpec((tm, D_MODEL), rows.blk(base_tile=x_base_tile)),
            _mod_spec(rows),
            pl.BlockSpec((1, D_MODEL), lambda t: (0, 0)),
            pl.BlockSpec((D_MODEL, 128), lambda t: (0, 0)),
            pl.BlockSpec((D_MODEL, 128), lambda t: (0, 0)),
        ],
        out_specs=(pl.BlockSpec((tm, D_MODEL), out_blk),
                   pl.BlockSpec((tm, 128), out_blk),
                   pl.BlockSpec((tm, 128), out_blk)),
        compiler_params=_cparams(("arbitrary",)),
        name="moe_router",
    )(xs, mod, gain, wr_hi, wr_lo)


MOE_TM = 512
MOE_TF = 512


def _expert_kernel(te_ref, ta_ref, src_ref, tok_ref, h_ref, wg_ref, wu_ref, wd_ref, o_ref,
                   hbuf, hb_scr, acc_scr, sem):
    t = pl.program_id(0)
    f = pl.program_id(1)
    active = ta_ref[t] == 1

    nf = D_FF_EXPERT // MOE_TF
    per_step = MOE_TM // nf
    slot = t % 2
    last_sorted = tok_ref.shape[0] - 1

    def start_row(tile, sl, r):
        tok = tok_ref[jnp.minimum(src_ref[tile] + r, last_sorted)]
        pltpu.make_async_copy(h_ref.at[pl.ds(tok, 1)], hbuf.at[sl, pl.ds(r, 1)], sem.at[sl]).start()

    @pl.when(f == 0)
    def _():
        @pl.when(t == 0)
        def _():
            def body(r, carry):
                start_row(0, 0, r)
                return carry
            lax.fori_loop(0, MOE_TM, body, 0, unroll=8)

        requested = jnp.logical_or(t == 0, ta_ref[jnp.maximum(t - 1, 0)] == 1)

        @pl.when(requested)
        def _():
            pltpu.make_async_copy(h_ref.at[pl.ds(0, MOE_TM)], hbuf.at[slot], sem.at[slot]).wait()

        @pl.when(active)
        def _():
            hb_scr[...] = hbuf[slot].astype(BF16)
            for r in range(per_step * nf, MOE_TM):
                start_row(t + 1, 1 - slot, r)

        acc_scr[...] = jnp.zeros_like(acc_scr)

    @pl.when(active)
    def _():
        for r in range(per_step):
            start_row(t + 1, 1 - slot, f * per_step + r)
        h = hb_scr[...]
        hid = (_silu(_dot(h, wg_ref[...])) * _dot(h, wu_ref[...])).astype(BF16)
        acc_scr[...] += _dot(hid, wd_ref[...])

    @pl.when(f == pl.num_programs(1) - 1)
    def _():
        o_ref[...] = acc_scr[...]


def _experts(tile_expert, tile_active, tile_src, sorted_tok, h, wg_bf, wu_bf, wd_bf, layer):
    nslot = tile_expert.shape[0] * MOE_TM
    nf = D_FF_EXPERT // MOE_TF

    def fcol(t, f, ta):
        return jnp.where(ta[t] == 1, f, nf - 1)
    return pl.pallas_call(
        _expert_kernel,
        out_shape=jax.ShapeDtypeStruct((nslot, D_MODEL), F32),
        grid_spec=pltpu.PrefetchScalarGridSpec(
            num_scalar_prefetch=4,
            grid=(nslot // MOE_TM, nf),
            in_specs=[
                pl.BlockSpec(memory_space=pl.ANY),
                pl.BlockSpec((None, None, D_MODEL, MOE_TF),
                             lambda t, f, te, ta, ts, tk: (layer, te[t], 0, fcol(t, f, ta))),
                pl.BlockSpec((None, None, D_MODEL, MOE_TF),
                             lambda t, f, te, ta, ts, tk: (layer, te[t], 0, fcol(t, f, ta))),
                pl.BlockSpec((None, None, MOE_TF, D_MODEL),
                             lambda t, f, te, ta, ts, tk: (layer, te[t], fcol(t, f, ta), 0)),
            ],
            out_specs=pl.BlockSpec((MOE_TM, D_MODEL), lambda t, f, te, ta, ts, tk: (t, 0)),
            scratch_shapes=[pltpu.VMEM((2, MOE_TM, D_MODEL), F32),
                            pltpu.VMEM((MOE_TM, D_MODEL), BF16),
                            pltpu.VMEM((MOE_TM, D_MODEL), F32),
                            pltpu.SemaphoreType.DMA((2,))],
        ),
        compiler_params=_cparams(("arbitrary", "arbitrary")),
        name="moe_experts",
    )(tile_expert, tile_active, tile_src, sorted_tok, h, wg_bf, wu_bf, wd_bf)


COMBINE_TM = 256


def _combine_kernel(pos_ref, ys_ref, x_ref, wt_ref, mod_ref, fg_ref, o_ref, buf, sem, *, final):
    t = pl.program_id(0)
    nt = pl.num_programs(0)
    slot = t % 2

    def start_tile(tile, sl):
        base = tile * COMBINE_TM

        def body(r, carry):
            for k in range(2):
                p = pos_ref[2 * (base + r) + k]
                pltpu.make_async_copy(ys_ref.at[pl.ds(p, 1)], buf.at[sl, k, pl.ds(r, 1)],
                                      sem.at[sl]).start()
            return carry
        lax.fori_loop(0, COMBINE_TM, body, 0, unroll=4)

    @pl.when(t == 0)
    def _():
        start_tile(0, 0)

    @pl.when(t + 1 < nt)
    def _():
        start_tile(t + 1, 1 - slot)

    for k in range(2):
        pltpu.make_async_copy(ys_ref.at[pl.ds(0, COMBINE_TM)], buf.at[slot, k], sem.at[slot]).wait()
    w = wt_ref[...]
    f = w[:, 0:1] * buf[slot, 0] + w[:, 1:2] * buf[slot, 1]
    y = x_ref[...] + mod_ref[5] * f
    if final:
        ms = jnp.mean(y * y, axis=-1, keepdims=True)
        y = y * lax.rsqrt(ms + EPS) * fg_ref[...]
    o_ref[...] = y


def _combine(pos, ys, xs, wts, mod, final_g, rows, x_base_tile, final):
    tm = rows.tm
    nrow = rows.n * tm
    return pl.pallas_call(
        functools.partial(_combine_kernel, final=final),
        out_shape=jax.ShapeDtypeStruct((nrow, D_MODEL), F32),
        grid_spec=pltpu.PrefetchScalarGridSpec(
            num_scalar_prefetch=1,
            grid=(rows.n,),
            in_specs=[
                pl.BlockSpec(memory_space=pl.ANY),
                pl.BlockSpec((tm, D_MODEL), rows.blk(base_tile=x_base_tile)),
                pl.BlockSpec((tm, 128), lambda t, p: (t, 0)),
                _mod_spec(rows),
                pl.BlockSpec((1, D_MODEL), lambda t, p: (0, 0)),
            ],
            out_specs=pl.BlockSpec((tm, D_MODEL), lambda t, p: (t, 0)),
            scratch_shapes=[pltpu.VMEM((2, 2, tm, D_MODEL), F32), pltpu.SemaphoreType.DMA((2,))],
        ),
        compiler_params=_cparams(("arbitrary",)),
        name="moe_combine",
    )(pos, ys, xs, wts, mod, final_g)


def _routing_tables(idx, nslot_pad):
    nrow = idx.shape[0]
    e_flat = idx[:, :2].reshape(-1)
    onehot = (e_flat[:, None] == jnp.arange(N_EXPERTS, dtype=jnp.int32)[None, :]).astype(jnp.int32)
    csum = jnp.cumsum(onehot, axis=0)
    rank = jnp.sum(csum * onehot, axis=1) - 1
    counts = csum[-1]
    padded = ((counts + MOE_TM - 1) // MOE_TM) * MOE_TM
    ends = jnp.cumsum(padded)
    offs = ends - padded
    pos = (jnp.sum(onehot * offs[None, :], axis=1) + rank).astype(jnp.int32)
    nslot = 2 * nrow
    assert nslot <= 1 << 16
    keys = jnp.sort(e_flat * (1 << 16) + jnp.arange(nslot, dtype=jnp.int32))
    sorted_tok = ((keys & 0xFFFF) >> 1).astype(jnp.int32)
    ntile = nslot_pad // MOE_TM
    tile_start = jnp.arange(ntile, dtype=jnp.int32) * MOE_TM
    tile_active = (tile_start < ends[-1]).astype(jnp.int32)
    last_start = jnp.maximum(ends[-1] - MOE_TM, 0)
    tile_start = jnp.minimum(tile_start, last_start)
    tile_expert = jnp.sum((tile_start[:, None] >= ends[None, :]).astype(jnp.int32), axis=1)
    tile_expert = jnp.minimum(tile_expert, N_EXPERTS - 1).astype(jnp.int32)
    te_onehot = (tile_expert[:, None] == jnp.arange(N_EXPERTS, dtype=jnp.int32)[None, :]).astype(jnp.int32)
    group_start = jnp.cumsum(counts) - counts
    tile_src = (jnp.sum(te_onehot * (group_start - offs)[None, :], axis=1) + tile_start).astype(jnp.int32)
    return pos, sorted_tok, tile_src, tile_expert, tile_active


def _moe(xs, mod, gain, wr, wg_bf, wu_bf, wd_bf, layer, final_g, rows, x_base_tile, final):
    wr_pad = jnp.zeros((D_MODEL, 128), F32).at[:, :N_EXPERTS].set(wr)
    wr_hi = wr_pad.astype(BF16)
    wr_lo = (wr_pad - wr_hi.astype(F32)).astype(BF16)
    h, idx, wts = _router(xs, mod, gain, wr_hi, wr_lo, rows, x_base_tile)
    nrow = h.shape[0]
    nslot_pad = 2 * nrow + N_EXPERTS * MOE_TM
    pos, sorted_tok, tile_src, tile_expert, tile_active = _routing_tables(idx, nslot_pad)
    ys = _experts(tile_expert, tile_active, tile_src, sorted_tok, h, wg_bf, wu_bf, wd_bf, layer)
    return _combine(pos, ys, xs, wts, mod, final_g, rows, x_base_tile, final)


def kernel(x, c, ctx, c_ctx, ada_w, ada_b, norm_mix_g, norm_ffn_g, final_g, ret_w_in, ret_log_decay,
           ret_gn_w, ret_gn_b, ret_w_out, gla_w_in, gla_w_gate_up, gla_b_gate, gla_norm_g, gla_w_out,
           ffn_w_gate, ffn_w_up, ffn_w_down, moe_w_router, moe_w_gate, moe_w_up, moe_w_down):
    batch, seq, d = x.shape
    assert d == D_MODEL and ctx.shape == (batch, CTX_LEN, d) and seq % 1024 == 0
    nctx = batch * CTX_LEN
    xs = jnp.concatenate([ctx.reshape(nctx, d), x.reshape(batch * seq, d)], axis=0)
    mod_all = _modulation(c, c_ctx, ada_w, ada_b)

    proj_tm = 512
    cos, sin = _rope_tables(seq, proj_tm)
    ret_colscale = jnp.concatenate([
        jnp.ones((1, RET_QK + RET_V), F32),
        jnp.full((1, RET_QK), RET_DK ** -0.5, F32),
        jnp.ones((1, RET_V), F32)], axis=1)
    gla_main = 2 * GLA_K + 2 * GLA_V
    gla_colscale = jnp.concatenate([
        jnp.full((1, GLA_K), GLA_DK ** -0.5, F32),
        jnp.ones((1, gla_main - GLA_K), F32)], axis=1)

    final_row = final_g.reshape(1, d)
    moe_gate_bf = moe_w_gate.astype(BF16)
    moe_up_bf = moe_w_up.astype(BF16)
    moe_down_bf = moe_w_down.astype(BF16)
    for i in range(DEPTH):
        last = i == DEPTH - 1
        j = i // 2
        mod = mod_all[i]
        rows_all_proj = _Rows(batch, seq, proj_tm, False)
        rows_all = _Rows(batch, seq, 512, False)
        rows_lat = _Rows(batch, seq, 512, True)
        mix_gain = norm_mix_g[i].reshape(1, d)
        ffn_gain = norm_ffn_g[i].reshape(1, d)
        if i % 2 == 0:
            p = _ret_proj(xs, mod, mix_gain, ret_w_in[j].astype(BF16), ret_colscale, cos, sin,
                          rows_all_proj)
            ld = jnp.broadcast_to(ret_log_decay[j].reshape(2 * RET_HEADS, 1, 1), (2 * RET_HEADS, 1, 128))
            o_f, o_b = _ret_scan(p, ld, batch, seq)
            xs = _ret_out(o_f, o_b, p, xs, mod, ret_gn_w[j].reshape(1, RET_V),
                          ret_gn_b[j].reshape(1, RET_V), ret_w_out[j].astype(BF16), rows_all)
            xs = _ffn(xs, mod, ffn_gain, ffn_w_gate[j].astype(BF16), ffn_w_up[j].astype(BF16),
                      ffn_w_down[j].astype(BF16), rows_all)
        else:
            w_in = gla_w_in[j]
            wa = jnp.zeros((d, 128), F32).at[:, :2 * GLA_GATE_RANK].set(w_in[:, gla_main:])
            wbd = jnp.zeros((128, 2 * GLA_K), F32)
            wbd = wbd.at[:GLA_GATE_RANK, :GLA_K].set(gla_w_gate_up[j, 0])
            wbd = wbd.at[GLA_GATE_RANK:2 * GLA_GATE_RANK, GLA_K:].set(gla_w_gate_up[j, 1])
            p, lg = _gla_proj(xs, mod, mix_gain, w_in[:, :gla_main].astype(BF16), gla_colscale,
                              wa.astype(BF16), wbd.astype(BF16), gla_b_gate[j].reshape(1, 2 * GLA_K),
                              rows_all_proj)
            o_f, o_b = _gla_scan(p, lg, batch, seq)
            rows = rows_lat if last else rows_all
            base = rows.start if last else 0
            xs = _gla_out(o_f, o_b, p, xs, mod, gla_norm_g[j].reshape(1, GLA_V),
                          gla_w_out[j].astype(BF16), rows, base, rows.n * rows.tm)
            rows_c = _Rows(batch, seq, COMBINE_TM, last)
            base_c = rows_c.start if last else 0
            xs = _moe(xs, mod, ffn_gain, moe_w_router[j], moe_gate_bf, moe_up_bf, moe_down_bf, j,
                      final_row, rows_c, base_c, last)
    return xs.reshape(batch, seq, d)
```

```python
import functools

import jax
import jax.numpy as jnp
from jax import lax
from jax.experimental import pallas as pl
from jax.experimental.pallas import tpu as pltpu

D_MODEL = 1024
GRID_W = 64
CTX_LEN = 256
DEPTH = 4
RET_HEADS = 4
RET_DK = 256
RET_DV = 512
RET_QK = 1024
RET_V = 2048
GLA_HEADS = 4
GLA_DK = 128
GLA_DV = 256
GLA_K = 512
GLA_V = 1024
GLA_GATE_RANK = 16
GLA_TAU = 16.0
D_FF = 2816
N_EXPERTS = 8
D_FF_EXPERT = 3584
ROPE_BASE = 10000.0
EPS = 1e-6

BF16 = jnp.bfloat16
F32 = jnp.float32

SCAN_ROWS = 256
GLA_SUB = 128
GLA_BLK = 16
VMEM_LIMIT = 48 * 1024 * 1024


def _cparams(sem):
    return pltpu.CompilerParams(dimension_semantics=sem, vmem_limit_bytes=VMEM_LIMIT)


def _dot(a, b):
    return jnp.dot(a, b, preferred_element_type=F32)


def _dot_nt(a, b):
    return lax.dot_general(a, b, (((1,), (1,)), ((), ())), preferred_element_type=F32)


def _dot_tn(a, b):
    return lax.dot_general(a, b, (((0,), (0,)), ((), ())), preferred_element_type=F32)


def _silu(x):
    return x * jax.nn.sigmoid(x)


def _norm_mod(x, gain, mod_ref, shift_i, scale_i):
    ms = jnp.mean(x * x, axis=-1, keepdims=True)
    y = x * lax.rsqrt(ms + EPS) * gain
    return y * (1.0 + mod_ref[scale_i]) + mod_ref[shift_i]


class _Rows:
    def __init__(self, batch, seq, tm, lat_only):
        self.tm = tm
        self.nct = batch * CTX_LEN // tm
        self.tpb = seq // tm
        self.start = self.nct if lat_only else 0
        self.n = batch * seq // tm + (0 if lat_only else self.nct)
        self.batch = batch

    def blk(self, base_tile=0, col=0):
        off = self.start - base_tile
        return lambda t, *_: (t + off, col)

    def mod_blk(self):
        start, nct, tpb, batch = self.start, self.nct, self.tpb, self.batch

        def im(t, *_):
            tg = t + start
            return (jnp.where(tg < nct, batch, (tg - nct) // tpb), 0, 0, 0)
        return im

    def pos_blk(self):
        start, nct, tpb = self.start, self.nct, self.tpb

        def im(t, *_):
            tg = t + start
            return (jnp.where(tg < nct, 0, 1 + (tg - nct) % tpb), 0)
        return im


def _mod_spec(rows):
    return pl.BlockSpec((None, 6, 1, D_MODEL), rows.mod_blk())


def _mod_kernel(c_ref, w_ref, b_ref, o_ref):
    s = _silu(c_ref[...]).astype(BF16)
    o_ref[...] = _dot(s, w_ref[...].astype(BF16)) + b_ref[...]


def _modulation(c, c_ctx, ada_w, ada_b):
    batch = c.shape[0]
    nrow = 16
    cc = jnp.concatenate([c, c_ctx[None], jnp.zeros((nrow - batch - 1, D_MODEL), F32)], axis=0)
    out = pl.pallas_call(
        _mod_kernel,
        out_shape=jax.ShapeDtypeStruct((DEPTH, 6, nrow, D_MODEL), F32),
        grid=(DEPTH, 6),
        in_specs=[
            pl.BlockSpec((nrow, D_MODEL), lambda i, k: (0, 0)),
            pl.BlockSpec((None, D_MODEL, D_MODEL), lambda i, k: (i, 0, k)),
            pl.BlockSpec((None, None, 1, D_MODEL), lambda i, k: (i, k, 0, 0)),
        ],
        out_specs=pl.BlockSpec((None, None, nrow, D_MODEL), lambda i, k: (i, k, 0, 0)),
        compiler_params=_cparams(("arbitrary", "arbitrary")),
        name="adaln_mod",
    )(cc, ada_w, ada_b.reshape(DEPTH, 6, 1, D_MODEL))
    return out.transpose(0, 2, 1, 3)[:, :, :, None, :]


PROJ_TN = 1024
PROJ_RC = 256


def _ret_proj_kernel(x_ref, mod_ref, gain_ref, w_ref, cs_ref, cos_ref, sin_ref, o_ref):
    rope_tiles = (0, (RET_QK + RET_V) // PROJ_TN)
    for r in range(x_ref.shape[0] // PROJ_RC):
        rs = slice(r * PROJ_RC, (r + 1) * PROJ_RC)
        h = _norm_mod(x_ref[rs, :], gain_ref[...], mod_ref, 0, 1).astype(BF16)
        for j in range(w_ref.shape[1] // PROJ_TN):
            cols = slice(j * PROJ_TN, (j + 1) * PROJ_TN)
            acc = _dot(h, w_ref[:, cols]) * cs_ref[:, cols]
            if j not in rope_tiles:
                o_ref[rs, cols] = acc.astype(BF16)
                continue
            for blk in range(PROJ_TN // 128):
                lo = blk * 128
                tl = (blk % 2) * 128
                a = acc[:, lo:lo + 128]
                rot = pltpu.roll(a, 64, 1)
                o_ref[rs, j * PROJ_TN + lo:j * PROJ_TN + lo + 128] = (
                    a * cos_ref[rs, tl:tl + 128] + rot * sin_ref[rs, tl:tl + 128]).astype(BF16)


def _rope_tables(seq, tm):
    half = RET_DK // 4
    freqs = ROPE_BASE ** (-jnp.arange(half, dtype=F32) / half)
    rows = seq // GRID_W
    row = jnp.broadcast_to(jnp.arange(rows, dtype=F32)[:, None], (rows, GRID_W)).reshape(seq)
    col = jnp.broadcast_to(jnp.arange(GRID_W, dtype=F32)[None, :], (rows, GRID_W)).reshape(seq)
    ar = row[:, None] * freqs[None, :]
    ac = col[:, None] * freqs[None, :]
    cos = jnp.concatenate([jnp.cos(ar), jnp.cos(ar), jnp.cos(ac), jnp.cos(ac)], axis=-1)
    sin = jnp.concatenate([-jnp.sin(ar), jnp.sin(ar), -jnp.sin(ac), jnp.sin(ac)], axis=-1)
    cos = jnp.concatenate([jnp.ones((tm, RET_DK), F32), cos], axis=0)
    sin = jnp.concatenate([jnp.zeros((tm, RET_DK), F32), sin], axis=0)
    return cos, sin


def _resident(shape):
    return pl.BlockSpec(shape, lambda *_: (0,) * len(shape), pipeline_mode=pl.Buffered(1))


def _ret_proj(xs, mod, gain, w_bf, colscale, cos, sin, rows):
    tm = rows.tm
    nout = w_bf.shape[1]
    return pl.pallas_call(
        _ret_proj_kernel,
        out_shape=jax.ShapeDtypeStruct((xs.shape[0], nout), BF16),
        grid=(rows.n,),
        in_specs=[
            pl.BlockSpec((tm, D_MODEL), rows.blk()),
            _mod_spec(rows),
            _resident((1, D_MODEL)),
            _resident((D_MODEL, nout)),
            _resident((1, nout)),
            pl.BlockSpec((tm, RET_DK), rows.pos_blk()),
            pl.BlockSpec((tm, RET_DK), rows.pos_blk()),
        ],
        out_specs=pl.BlockSpec((tm, nout), rows.blk()),
        compiler_params=_cparams(("arbitrary",)),
        name="ret_proj",
    )(xs, mod, gain, w_bf, colscale, cos, sin)


def _log_sigmoid(z):
    return jnp.minimum(z, 0.0) - jnp.log1p(jnp.exp(-jnp.abs(z)))


def _gla_proj_kernel(x_ref, mod_ref, gain_ref, w_ref, cs_ref, wa_ref, wbd_ref, bg_ref,
                     o_ref, lg_ref):
    for r in range(x_ref.shape[0] // PROJ_RC):
        rs = slice(r * PROJ_RC, (r + 1) * PROJ_RC)
        h = _norm_mod(x_ref[rs, :], gain_ref[...], mod_ref, 0, 1).astype(BF16)
        a = _dot(h, wa_ref[...]).astype(BF16)
        z = _dot(a, wbd_ref[...]) + bg_ref[...]
        lg_ref[rs, :] = _log_sigmoid(z) * (1.0 / GLA_TAU)
        for j in range(w_ref.shape[1] // PROJ_TN):
            cols = slice(j * PROJ_TN, (j + 1) * PROJ_TN)
            o_ref[rs, cols] = (_dot(h, w_ref[:, cols]) * cs_ref[:, cols]).astype(BF16)


def _gla_proj(xs, mod, gain, w_bf, colscale, wa_bf, wbd_bf, bgate, rows):
    tm = rows.tm
    nout = w_bf.shape[1]
    n = xs.shape[0]
    return pl.pallas_call(
        _gla_proj_kernel,
        out_shape=(jax.ShapeDtypeStruct((n, nout), BF16),
                   jax.ShapeDtypeStruct((n, 2 * GLA_K), F32)),
        grid=(rows.n,),
        in_specs=[
            pl.BlockSpec((tm, D_MODEL), rows.blk()),
            _mod_spec(rows),
            _resident((1, D_MODEL)),
            _resident((D_MODEL, nout)),
            _resident((1, nout)),
            _resident((D_MODEL, 128)),
            _resident((128, 2 * GLA_K)),
            _resident((1, 2 * GLA_K)),
        ],
        out_specs=(pl.BlockSpec((tm, nout), rows.blk()),
                   pl.BlockSpec((tm, 2 * GLA_K), rows.blk())),
        compiler_params=_cparams(("arbitrary",)),
        name="gla_proj",
    )(xs, mod, gain, w_bf, colscale, wa_bf, wbd_bf, bgate)


def _scan_row_maps(batch, seq):
    nlat = seq // SCAN_ROWS
    nctx_blocks = batch * CTX_LEN // SCAN_ROWS

    def fwd(b, c):
        return jnp.where(c == 0, b, nctx_blocks + b * nlat + c - 1)

    def bwd(b, c):
        return jnp.where(c == 0, b, nctx_blocks + b * nlat + nlat - c)
    return fwd, bwd, nlat + 1


def _ret_scan_kernel(ld_ref, qf_ref, kf_ref, vf_ref, qb_ref, kb_ref, vb_ref,
                     of_ref, ob_ref, sf_ref, sb_ref, dm_ref, qd_ref, kd_ref, cd_ref):
    b = pl.program_id(0)
    c = pl.program_id(1)
    n = SCAN_ROWS

    @pl.when(jnp.logical_and(b == 0, c == 0))
    def _():
        ii = lax.broadcasted_iota(jnp.int32, (n, n), 0).astype(F32)
        jj = lax.broadcasted_iota(jnp.int32, (n, n), 1).astype(F32)
        for d in range(2):
            rev = d == 1
            for h in range(RET_HEADS):
                lg = -jnp.exp(ld_ref[d * RET_HEADS + h])[:, :1]
                diff = (jj - ii) if rev else (ii - jj)
                dm_ref[d, h] = jnp.where(diff >= 0, jnp.exp(lg * jnp.maximum(diff, 0.0)), 0.0)
                qd_ref[d, h] = jnp.exp(lg * ((n - ii) if rev else (ii + 1.0)))
                kd_ref[d, h] = jnp.exp(lg * (ii if rev else (n - 1.0 - ii)))
                cd_ref[d, h] = jnp.broadcast_to(jnp.exp(lg * float(n)), (8, 128))

    @pl.when(c == 0)
    def _():
        sf_ref[...] = jnp.zeros_like(sf_ref)
        sb_ref[...] = jnp.zeros_like(sb_ref)

    def one(d, h, q_ref, k_ref, v_ref, o_ref, s_ref):
        qk = slice(h * RET_DK, (h + 1) * RET_DK)
        vv = slice(h * RET_DV, (h + 1) * RET_DV)
        q = q_ref[:, qk]
        k = k_ref[:, qk]
        v = v_ref[:, vv]
        scores = _dot_nt(q, k) * dm_ref[d, h]
        s = s_ref[h]
        o = _dot(scores.astype(BF16), v)
        o = o + _dot((q.astype(F32) * qd_ref[d, h]).astype(BF16), s.astype(BF16))
        o_ref[:, vv] = o.astype(o_ref.dtype)
        s_ref[h] = s * cd_ref[d, h][:1, :1] + _dot_tn((k.astype(F32) * kd_ref[d, h]).astype(BF16), v)

    for h in range(RET_HEADS):
        one(0, h, qf_ref, kf_ref, vf_ref, of_ref, sf_ref)
        one(1, h, qb_ref, kb_ref, vb_ref, ob_ref, sb_ref)


def _ret_scan(p, ld, batch, seq):
    fwd, bwd, nsteps = _scan_row_maps(batch, seq)
    n = p.shape[0]
    kcol = (RET_QK + RET_V) // RET_QK
    vcol = (2 * RET_QK + RET_V) // RET_V
    R = SCAN_ROWS

    def specs(rm):
        return [
            pl.BlockSpec((R, RET_QK), lambda b, c: (rm(b, c), 0)),
            pl.BlockSpec((R, RET_QK), lambda b, c: (rm(b, c), kcol)),
            pl.BlockSpec((R, RET_V), lambda b, c: (rm(b, c), vcol)),
        ]
    state = pltpu.VMEM((RET_HEADS, RET_DK, RET_DV), F32)
    table = pltpu.VMEM((2, RET_HEADS, R, R), F32)
    return pl.pallas_call(
        _ret_scan_kernel,
        out_shape=(jax.ShapeDtypeStruct((n, RET_V), BF16),
                   jax.ShapeDtypeStruct((n, RET_V), BF16)),
        grid=(batch, nsteps),
        in_specs=[pl.BlockSpec((2 * RET_HEADS, 1, 128), lambda b, c: (0, 0, 0))]
        + specs(fwd) + specs(bwd),
        out_specs=(pl.BlockSpec((R, RET_V), lambda b, c: (fwd(b, c), 0)),
                   pl.BlockSpec((R, RET_V), lambda b, c: (bwd(b, c), 0))),
        scratch_shapes=[state, state, table, table, table,
                        pltpu.VMEM((2, RET_HEADS, 8, 128), F32)],
        compiler_params=_cparams(("arbitrary", "arbitrary")),
        name="ret_scan",
    )(ld, p, p, p, p, p, p)


def _gla_tri(rev):
    n = GLA_SUB
    ri = lax.broadcasted_iota(jnp.int32, (n, n), 0)
    ci = lax.broadcasted_iota(jnp.int32, (n, n), 1)
    return (ci >= ri) if rev else (ci <= ri)


def _gla_cum_gates(g_ref, i, rev):
    n = GLA_SUB
    g = g_ref[i * n:(i + 1) * n, :]
    tmat = jnp.where(_gla_tri(rev), 1.0, 0.0).astype(BF16)
    g_hi = g.astype(BF16)
    g_lo = (g - g_hi.astype(F32)).astype(BF16)
    return _dot(tmat, g_hi) + _dot(tmat, g_lo)


def _gla_pairs(rev):
    nb = GLA_SUB // GLA_BLK
    return [(bj, bi) for bj in range(nb) for bi in (range(bj, nb) if rev else range(bj + 1))]


def _gla_operands(q_ref, k_ref, v_ref, b_all, h, i, rev):
    n, r = GLA_SUB, GLA_BLK
    nb = n // r
    rows = slice(i * n, (i + 1) * n)
    kcols = slice(h * GLA_DK, (h + 1) * GLA_DK)
    q = q_ref[rows, kcols].astype(F32)
    k = k_ref[rows, kcols].astype(F32)
    v = v_ref[rows, h * GLA_DV:(h + 1) * GLA_DV]
    b = b_all[:, kcols]
    b_last = b[0:1] if rev else b[n - 1:n]
    q_in = (q * jnp.exp(b)).astype(BF16)
    k_out = (k * jnp.exp(b_last - b)).astype(BF16)
    refs = [b[bi * r:bi * r + 1] if rev else b[bi * r + r - 1:bi * r + r] for bi in range(nb)]
    k_hat = jnp.concatenate(
        [k[bi * r:(bi + 1) * r] * jnp.exp(refs[bi] - b[bi * r:(bi + 1) * r]) for bi in range(nb)],
        axis=0).astype(BF16)
    q_cat = jnp.concatenate(
        [q[bj * r:(bj + 1) * r] * jnp.exp(b[bj * r:(bj + 1) * r] - refs[bi])
         for bj, bi in _gla_pairs(rev)], axis=0).astype(BF16)
    return dict(v=v, q_in=q_in, k_out=k_out, k_hat=k_hat, q_cat=q_cat, decay=jnp.exp(b_last))


def _gla_scores(raw, rev):
    n, r = GLA_SUB, GLA_BLK
    pairs = _gla_pairs(rev)
    col_blk = lax.broadcasted_iota(jnp.int32, (r, n), 1) // r
    row_blocks = []
    for bj in range(n // r):
        acc = jnp.zeros((r, n), F32)
        for gi, (pj, bi) in enumerate(pairs):
            if pj == bj:
                acc = jnp.where(col_blk == bi, raw[gi * r:(gi + 1) * r], acc)
        row_blocks.append(acc)
    return jnp.where(_gla_tri(rev), jnp.concatenate(row_blocks, axis=0), 0.0).astype(BF16)


def _gla_chunk_group(chains):
    ops = [_gla_operands(qr, kr, vr, b_all, h, i, rev) for (qr, kr, vr, b_all, _, h, i, rev) in chains]
    raws = [_dot_nt(o["q_cat"], o["k_hat"]) for o in ops]
    inter = [_dot_nt(o["q_in"], c[4].astype(BF16)) for o, c in zip(ops, chains)]
    kv = [_dot_tn(o["v"], o["k_out"]) for o in ops]
    scores = [_gla_scores(raw, c[7]) for raw, c in zip(raws, chains)]
    outs = [it + _dot(s, o["v"]) for it, s, o in zip(inter, scores, ops)]
    states = [c[4] * o["decay"] + x for c, o, x in zip(chains, ops, kv)]
    return outs, states


def _gla_scan_kernel(qf_ref, kf_ref, vf_ref, gf_ref, qb_ref, kb_ref, vb_ref, gb_ref,
                     of_ref, ob_ref, sf_ref, sb_ref):
    c = pl.program_id(1)
    nsub = SCAN_ROWS // GLA_SUB

    @pl.when(c == 0)
    def _():
        sf_ref[...] = jnp.zeros_like(sf_ref)
        sb_ref[...] = jnp.zeros_like(sb_ref)

    st_f = [sf_ref[h] for h in range(GLA_HEADS)]
    st_b = [sb_ref[h] for h in range(GLA_HEADS)]
    n = GLA_SUB
    for i in range(nsub):
        ib = nsub - 1 - i
        b_f = _gla_cum_gates(gf_ref, i, False)
        b_b = _gla_cum_gates(gb_ref, ib, True)
        chains = []
        for h in range(GLA_HEADS):
            chains.append((qf_ref, kf_ref, vf_ref, b_f, st_f[h], h, i, False))
            chains.append((qb_ref, kb_ref, vb_ref, b_b, st_b[h], h, ib, True))
        outs, states = _gla_chunk_group(chains)
        for h in range(GLA_HEADS):
            vcols = slice(h * GLA_DV, (h + 1) * GLA_DV)
            of_ref[i * n:(i + 1) * n, vcols] = outs[2 * h].astype(of_ref.dtype)
            ob_ref[ib * n:(ib + 1) * n, vcols] = outs[2 * h + 1].astype(ob_ref.dtype)
            st_f[h] = states[2 * h]
            st_b[h] = states[2 * h + 1]
    for h in range(GLA_HEADS):
        sf_ref[h] = st_f[h]
        sb_ref[h] = st_b[h]


def _gla_scan(p, lg, batch, seq):
    fwd, bwd, nsteps = _scan_row_maps(batch, seq)
    n = p.shape[0]
    kcol = (GLA_K + GLA_V) // GLA_K
    vcol = (2 * GLA_K + GLA_V) // GLA_V
    R = SCAN_ROWS

    def specs(rm, gcol):
        return [
            pl.BlockSpec((R, GLA_K), lambda b, c: (rm(b, c), 0)),
            pl.BlockSpec((R, GLA_K), lambda b, c: (rm(b, c), kcol)),
            pl.BlockSpec((R, GLA_V), lambda b, c: (rm(b, c), vcol)),
            pl.BlockSpec((R, GLA_K), lambda b, c: (rm(b, c), gcol)),
        ]
    state = pltpu.VMEM((GLA_HEADS, GLA_DV, GLA_DK), F32)
    return pl.pallas_call(
        _gla_scan_kernel,
        out_shape=(jax.ShapeDtypeStruct((n, GLA_V), BF16),
                   jax.ShapeDtypeStruct((n, GLA_V), BF16)),
        grid=(batch, nsteps),
        in_specs=specs(fwd, 0) + specs(bwd, 1),
        out_specs=(pl.BlockSpec((R, GLA_V), lambda b, c: (fwd(b, c), 0)),
                   pl.BlockSpec((R, GLA_V), lambda b, c: (bwd(b, c), 0))),
        scratch_shapes=[state, state],
        compiler_params=_cparams(("arbitrary", "arbitrary")),
        name="gla_scan",
    )(p, p, p, lg, p, p, p, lg)


def _ret_out_kernel(of_ref, ob_ref, g0_ref, g1_ref, x_ref, mod_ref, gnw_ref, gnb_ref, w_ref, o_ref):
    acc = jnp.zeros(o_ref.shape, F32)
    for h in range(RET_HEADS):
        sl = slice(h * RET_DV, (h + 1) * RET_DV)
        o = of_ref[:, sl].astype(F32) + ob_ref[:, sl].astype(F32)
        mu = jnp.mean(o, axis=-1, keepdims=True)
        d = o - mu
        var = jnp.mean(d * d, axis=-1, keepdims=True)
        y = d * lax.rsqrt(var + EPS) * gnw_ref[:, sl] + gnb_ref[:, sl]
        g_ref = g0_ref if h < 2 else g1_ref
        g = g_ref[:, (h % 2) * RET_DV:(h % 2 + 1) * RET_DV].astype(F32)
        acc = acc + _dot((y * _silu(g)).astype(BF16), w_ref[sl, :])
    o_ref[...] = x_ref[...] + mod_ref[2] * acc


def _ret_out(o_f, o_b, p, xs, mod, gn_w, gn_b, w_bf, rows):
    tm = rows.tm
    return pl.pallas_call(
        _ret_out_kernel,
        out_shape=jax.ShapeDtypeStruct(xs.shape, F32),
        grid=(rows.n,),
        in_specs=[
            pl.BlockSpec((tm, RET_V), rows.blk()),
            pl.BlockSpec((tm, RET_V), rows.blk()),
            pl.BlockSpec((tm, 1024), rows.blk(col=1)),
            pl.BlockSpec((tm, 1024), rows.blk(col=2)),
            pl.BlockSpec((tm, D_MODEL), rows.blk()),
            _mod_spec(rows),
            pl.BlockSpec((1, RET_V), lambda t: (0, 0)),
            pl.BlockSpec((1, RET_V), lambda t: (0, 0)),
            pl.BlockSpec((RET_V, D_MODEL), lambda t: (0, 0)),
        ],
        out_specs=pl.BlockSpec((tm, D_MODEL), rows.blk()),
        compiler_params=_cparams(("arbitrary",)),
        name="ret_out",
    )(o_f, o_b, p, p, xs, mod, gn_w, gn_b, w_bf)


def _gla_out_kernel(of_ref, ob_ref, r0_ref, r1_ref, x_ref, mod_ref, ng_ref, w_ref, o_ref):
    acc = jnp.zeros(o_ref.shape, F32)
    for h in range(GLA_HEADS):
        sl = slice(h * GLA_DV, (h + 1) * GLA_DV)
        o = of_ref[:, sl].astype(F32) + ob_ref[:, sl].astype(F32)
        y = o * lax.rsqrt(jnp.mean(o * o, axis=-1, keepdims=True) + EPS) * ng_ref[:, sl]
        r_ref = r0_ref if h < 2 else r1_ref
        r = r_ref[:, (h % 2) * GLA_DV:(h % 2 + 1) * GLA_DV].astype(F32)
        acc = acc + _dot((y * _silu(r)).astype(BF16), w_ref[sl, :])
    o_ref[...] = x_ref[...] + mod_ref[2] * acc


def _gla_out(o_f, o_b, p, xs, mod, norm_g, w_bf, rows, x_base_tile, out_rows):
    tm = rows.tm
    return pl.pallas_call(
        _gla_out_kernel,
        out_shape=jax.ShapeDtypeStruct((out_rows, D_MODEL), F32),
        grid=(rows.n,),
        in_specs=[
            pl.BlockSpec((tm, GLA_V), rows.blk()),
            pl.BlockSpec((tm, GLA_V), rows.blk()),
            pl.BlockSpec((tm, 512), rows.blk(col=1)),
            pl.BlockSpec((tm, 512), rows.blk(col=2)),
            pl.BlockSpec((tm, D_MODEL), rows.blk()),
            _mod_spec(rows),
            pl.BlockSpec((1, GLA_V), lambda t: (0, 0)),
            pl.BlockSpec((GLA_V, D_MODEL), lambda t: (0, 0)),
        ],
        out_specs=pl.BlockSpec((tm, D_MODEL), rows.blk(base_tile=x_base_tile)),
        compiler_params=_cparams(("arbitrary",)),
        name="gla_out",
    )(o_f, o_b, p, p, xs, mod, norm_g, w_bf)


def _ffn_kernel(x_ref, mod_ref, gain_ref, wg_ref, wu_ref, wd_ref, o_ref):
    for r in range(x_ref.shape[0] // PROJ_RC):
        rs = slice(r * PROJ_RC, (r + 1) * PROJ_RC)
        x = x_ref[rs, :]
        h = _norm_mod(x, gain_ref[...], mod_ref, 3, 4).astype(BF16)
        hid = (_silu(_dot(h, wg_ref[...])) * _dot(h, wu_ref[...])).astype(BF16)
        o_ref[rs, :] = x + mod_ref[5] * _dot(hid, wd_ref[...])


def _ffn(xs, mod, gain, wg_bf, wu_bf, wd_bf, rows):
    tm = rows.tm
    return pl.pallas_call(
        _ffn_kernel,
        out_shape=jax.ShapeDtypeStruct(xs.shape, F32),
        grid=(rows.n,),
        in_specs=[
            pl.BlockSpec((tm, D_MODEL), rows.blk()),
            _mod_spec(rows),
            _resident((1, D_MODEL)),
            _resident((D_MODEL, D_FF)),
            _resident((D_MODEL, D_FF)),
            _resident((D_FF, D_MODEL)),
        ],
        out_specs=pl.BlockSpec((tm, D_MODEL), rows.blk()),
        compiler_params=_cparams(("arbitrary",)),
        name="ffn",
    )(xs, mod, gain, wg_bf, wu_bf, wd_bf)


def _router_kernel(x_ref, mod_ref, gain_ref, wr_hi_ref, wr_lo_ref, h_ref, idx_ref, wt_ref):
    h = _norm_mod(x_ref[...], gain_ref[...], mod_ref, 3, 4)
    h_ref[...] = h
    h_hi = h.astype(BF16)
    h_lo = (h - h_hi.astype(F32)).astype(BF16)
    logits = _dot(h_hi, wr_hi_ref[...]) + _dot(h_lo, wr_hi_ref[...]) + _dot(h_hi, wr_lo_ref[...])
    lane = lax.broadcasted_iota(jnp.int32, logits.shape, 1)
    lane_f = lane.astype(F32)
    neg = jnp.float32(-jnp.inf)
    l1 = jnp.where(lane < N_EXPERTS, logits, neg)
    m1 = jnp.max(l1, axis=-1, keepdims=True)
    i1 = jnp.min(jnp.where(l1 == m1, lane_f, 128.0), axis=-1, keepdims=True)
    l2 = jnp.where(lane_f == i1, neg, l1)
    m2 = jnp.max(l2, axis=-1, keepdims=True)
    i2 = jnp.min(jnp.where(l2 == m2, lane_f, 128.0), axis=-1, keepdims=True)
    e2 = jnp.exp(m2 - m1)
    den = 1.0 + e2
    idx_ref[...] = jnp.where(lane == 0, i1, jnp.where(lane == 1, i2, 0.0)).astype(jnp.int32)
    wt_ref[...] = jnp.where(lane == 0, 1.0 / den, jnp.where(lane == 1, e2 / den, 0.0))


def _router(xs, mod, gain, wr_hi, wr_lo, rows, x_base_tile):
    tm = rows.tm
    nrow = rows.n * tm
    out_blk = lambda t: (t, 0)
    return pl.pallas_call(
        _router_kernel,
        out_shape=(jax.ShapeDtypeStruct((nrow, D_MODEL), F32),
                   jax.ShapeDtypeStruct((nrow, 128), jnp.int32),
                   jax.ShapeDtypeStruct((nrow, 128), F32)),
        grid=(rows.n,),
        in_specs=[
            pl.BlockSpec((tm, D_MODEL), rows.blk(base_tile=x_base_tile)),
            _mod_spec(rows),
            pl.BlockSpec((1, D_MODEL), lambda t: (0, 0)),
            pl.BlockSpec((D_MODEL, 128), lambda t: (0, 0)),
            pl.BlockSpec((D_MODEL, 128), lambda t: (0, 0)),
        ],
        out_specs=(pl.BlockSpec((tm, D_MODEL), out_blk),
                   pl.BlockSpec((tm, 128), out_blk),
                   pl.BlockSpec((tm, 128), out_blk)),
        compiler_params=_cparams(("arbitrary",)),
        name="moe_router",
    )(xs, mod, gain, wr_hi, wr_lo)


MOE_TM = 768
MOE_TF = 512


def _expert_kernel(te_ref, ta_ref, src_ref, tok_ref, h_ref, wg_ref, wu_ref, wd_ref, o_ref,
                   hbuf, hb_scr, acc_scr, sem):
    t = pl.program_id(0)
    f = pl.program_id(1)
    active = ta_ref[t] == 1

    nf = D_FF_EXPERT // MOE_TF
    per_step = MOE_TM // nf
    slot = t % 2
    last_sorted = tok_ref.shape[0] - 1

    def start_row(tile, sl, r):
        tok = tok_ref[jnp.minimum(src_ref[tile] + r, last_sorted)]
        pltpu.make_async_copy(h_ref.at[pl.ds(tok, 1)], hbuf.at[sl, pl.ds(r, 1)], sem.at[sl]).start()

    @pl.when(f == 0)
    def _():
        @pl.when(t == 0)
        def _():
            def body(r, carry):
                start_row(0, 0, r)
                return carry
            lax.fori_loop(0, MOE_TM, body, 0, unroll=8)

        requested = jnp.logical_or(t == 0, ta_ref[jnp.maximum(t - 1, 0)] == 1)

        @pl.when(requested)
        def _():
            pltpu.make_async_copy(h_ref.at[pl.ds(0, MOE_TM)], hbuf.at[slot], sem.at[slot]).wait()

        @pl.when(active)
        def _():
            hb_scr[...] = hbuf[slot].astype(BF16)
            for r in range(per_step * nf, MOE_TM):
                start_row(t + 1, 1 - slot, r)

        acc_scr[...] = jnp.zeros_like(acc_scr)

    @pl.when(active)
    def _():
        for r in range(per_step):
            start_row(t + 1, 1 - slot, f * per_step + r)
        h = hb_scr[...]
        gate = _dot(h, wg_ref[...].astype(BF16))
        up = _dot(h, wu_ref[...].astype(BF16))
        acc_scr[...] += _dot((_silu(gate) * up).astype(BF16), wd_ref[...].astype(BF16))

    @pl.when(f == pl.num_programs(1) - 1)
    def _():
        o_ref[...] = acc_scr[...]


def _experts(tile_expert, tile_active, tile_src, sorted_tok, h, wg_bf, wu_bf, wd_bf, layer):
    nslot = tile_expert.shape[0] * MOE_TM
    nf = D_FF_EXPERT // MOE_TF

    def fcol(t, f, ta):
        return jnp.where(ta[t] == 1, f, nf - 1)
    return pl.pallas_call(
        _expert_kernel,
        out_shape=jax.ShapeDtypeStruct((nslot, D_MODEL), F32),
        grid_spec=pltpu.PrefetchScalarGridSpec(
            num_scalar_prefetch=4,
            grid=(nslot // MOE_TM, nf),
            in_specs=[
                pl.BlockSpec(memory_space=pl.ANY),
                pl.BlockSpec((None, None, D_MODEL, MOE_TF),
                             lambda t, f, te, ta, ts, tk: (layer, te[t], 0, fcol(t, f, ta))),
                pl.BlockSpec((None, None, D_MODEL, MOE_TF),
                             lambda t, f, te, ta, ts, tk: (layer, te[t], 0, fcol(t, f, ta))),
                pl.BlockSpec((None, None, MOE_TF, D_MODEL),
                             lambda t, f, te, ta, ts, tk: (layer, te[t], fcol(t, f, ta), 0)),
            ],
            out_specs=pl.BlockSpec((MOE_TM, D_MODEL), lambda t, f, te, ta, ts, tk: (t, 0)),
            scratch_shapes=[pltpu.VMEM((2, MOE_TM, D_MODEL), F32),
                            pltpu.VMEM((MOE_TM, D_MODEL), BF16),
                            pltpu.VMEM((MOE_TM, D_MODEL), F32),
                            pltpu.SemaphoreType.DMA((2,))],
        ),
        compiler_params=_cparams(("arbitrary", "arbitrary")),
        name="moe_experts",
    )(tile_expert, tile_active, tile_src, sorted_tok, h, wg_bf, wu_bf, wd_bf)


COMBINE_TM = 256


def _combine_kernel(pos_ref, ys_ref, x_ref, wt_ref, mod_ref, fg_ref, o_ref, buf, sem, *, final):
    t = pl.program_id(0)
    nt = pl.num_programs(0)
    slot = t % 2

    def start_tile(tile, sl):
        base = tile * COMBINE_TM

        def body(r, carry):
            for k in range(2):
                p = pos_ref[2 * (base + r) + k]
                pltpu.make_async_copy(ys_ref.at[pl.ds(p, 1)], buf.at[sl, k, pl.ds(r, 1)],
                                      sem.at[sl]).start()
            return carry
        lax.fori_loop(0, COMBINE_TM, body, 0, unroll=4)

    @pl.when(t == 0)
    def _():
        start_tile(0, 0)

    @pl.when(t + 1 < nt)
    def _():
        start_tile(t + 1, 1 - slot)

    for k in range(2):
        pltpu.make_async_copy(ys_ref.at[pl.ds(0, COMBINE_TM)], buf.at[slot, k], sem.at[slot]).wait()
    w = wt_ref[...]
    f = w[:, 0:1] * buf[slot, 0] + w[:, 1:2] * buf[slot, 1]
    y = x_ref[...] + mod_ref[5] * f
    if final:
        ms = jnp.mean(y * y, axis=-1, keepdims=True)
        y = y * lax.rsqrt(ms + EPS) * fg_ref[...]
    o_ref[...] = y


def _combine(pos, ys, xs, wts, mod, final_g, rows, x_base_tile, final):
    tm = rows.tm
    nrow = rows.n * tm
    return pl.pallas_call(
        functools.partial(_combine_kernel, final=final),
        out_shape=jax.ShapeDtypeStruct((nrow, D_MODEL), F32),
        grid_spec=pltpu.PrefetchScalarGridSpec(
            num_scalar_prefetch=1,
            grid=(rows.n,),
            in_specs=[
                pl.BlockSpec(memory_space=pl.ANY),
                pl.BlockSpec((tm, D_MODEL), rows.blk(base_tile=x_base_tile)),
                pl.BlockSpec((tm, 128), lambda t, p: (t, 0)),
                _mod_spec(rows),
                pl.BlockSpec((1, D_MODEL), lambda t, p: (0, 0)),
            ],
            out_specs=pl.BlockSpec((tm, D_MODEL), lambda t, p: (t, 0)),
            scratch_shapes=[pltpu.VMEM((2, 2, tm, D_MODEL), F32), pltpu.SemaphoreType.DMA((2,))],
        ),
        compiler_params=_cparams(("arbitrary",)),
        name="moe_combine",
    )(pos, ys, xs, wts, mod, final_g)


def _routing_tables(idx, nslot_pad):
    nrow = idx.shape[0]
    e_flat = idx[:, :2].reshape(-1)
    onehot = (e_flat[:, None] == jnp.arange(N_EXPERTS, dtype=jnp.int32)[None, :]).astype(jnp.int32)
    csum = jnp.cumsum(onehot, axis=0)
    rank = jnp.sum(csum * onehot, axis=1) - 1
    counts = csum[-1]
    padded = ((counts + MOE_TM - 1) // MOE_TM) * MOE_TM
    ends = jnp.cumsum(padded)
    offs = ends - padded
    pos = (jnp.sum(onehot * offs[None, :], axis=1) + rank).astype(jnp.int32)
    nslot = 2 * nrow
    assert nslot <= 1 << 16
    keys = jnp.sort(e_flat * (1 << 16) + jnp.arange(nslot, dtype=jnp.int32))
    sorted_tok = ((keys & 0xFFFF) >> 1).astype(jnp.int32)
    ntile = nslot_pad // MOE_TM
    tile_start = jnp.arange(ntile, dtype=jnp.int32) * MOE_TM
    tile_active = (tile_start < ends[-1]).astype(jnp.int32)
    last_start = jnp.maximum(ends[-1] - MOE_TM, 0)
    tile_start = jnp.minimum(tile_start, last_start)
    tile_expert = jnp.sum((tile_start[:, None] >= ends[None, :]).astype(jnp.int32), axis=1)
    tile_expert = jnp.minimum(tile_expert, N_EXPERTS - 1).astype(jnp.int32)
    te_onehot = (tile_expert[:, None] == jnp.arange(N_EXPERTS, dtype=jnp.int32)[None, :]).astype(jnp.int32)
    group_start = jnp.cumsum(counts) - counts
    tile_src = (jnp.sum(te_onehot * (group_start - offs)[None, :], axis=1) + tile_start).astype(jnp.int32)
    return pos, sorted_tok, tile_src, tile_expert, tile_active


def _moe(xs, mod, gain, wr, wg_bf, wu_bf, wd_bf, layer, final_g, rows, x_base_tile, final):
    wr_pad = jnp.zeros((D_MODEL, 128), F32).at[:, :N_EXPERTS].set(wr)
    wr_hi = wr_pad.astype(BF16)
    wr_lo = (wr_pad - wr_hi.astype(F32)).astype(BF16)
    h, idx, wts = _router(xs, mod, gain, wr_hi, wr_lo, rows, x_base_tile)
    nrow = h.shape[0]
    ntile = (2 * nrow + N_EXPERTS * (MOE_TM - 1)) // MOE_TM + 1
    nslot_pad = ntile * MOE_TM
    pos, sorted_tok, tile_src, tile_expert, tile_active = _routing_tables(idx, nslot_pad)
    ys = _experts(tile_expert, tile_active, tile_src, sorted_tok, h, wg_bf, wu_bf, wd_bf, layer)
    return _combine(pos, ys, xs, wts, mod, final_g, rows, x_base_tile, final)


def kernel(x, c, ctx, c_ctx, ada_w, ada_b, norm_mix_g, norm_ffn_g, final_g, ret_w_in, ret_log_decay,
           ret_gn_w, ret_gn_b, ret_w_out, gla_w_in, gla_w_gate_up, gla_b_gate, gla_norm_g, gla_w_out,
           ffn_w_gate, ffn_w_up, ffn_w_down, moe_w_router, moe_w_gate, moe_w_up, moe_w_down):
    batch, seq, d = x.shape
    assert d == D_MODEL and ctx.shape == (batch, CTX_LEN, d) and seq % 1024 == 0
    nctx = batch * CTX_LEN
    xs = jnp.concatenate([ctx.reshape(nctx, d), x.reshape(batch * seq, d)], axis=0)
    mod_all = _modulation(c, c_ctx, ada_w, ada_b)

    proj_tm = 512
    cos, sin = _rope_tables(seq, proj_tm)
    ret_colscale = jnp.concatenate([
        jnp.ones((1, RET_QK + RET_V), F32),
        jnp.full((1, RET_QK), RET_DK ** -0.5, F32),
        jnp.ones((1, RET_V), F32)], axis=1)
    gla_main = 2 * GLA_K + 2 * GLA_V
    gla_colscale = jnp.concatenate([
        jnp.full((1, GLA_K), GLA_DK ** -0.5, F32),
        jnp.ones((1, gla_main - GLA_K), F32)], axis=1)

    final_row = final_g.reshape(1, d)
    for i in range(DEPTH):
        last = i == DEPTH - 1
        j = i // 2
        mod = mod_all[i]
        rows_all_proj = _Rows(batch, seq, proj_tm, False)
        rows_all = _Rows(batch, seq, 512, False)
        rows_lat = _Rows(batch, seq, 512, True)
        mix_gain = norm_mix_g[i].reshape(1, d)
        ffn_gain = norm_ffn_g[i].reshape(1, d)
        if i % 2 == 0:
            p = _ret_proj(xs, mod, mix_gain, ret_w_in[j].astype(BF16), ret_colscale, cos, sin,
                          rows_all_proj)
            ld = jnp.broadcast_to(ret_log_decay[j].reshape(2 * RET_HEADS, 1, 1), (2 * RET_HEADS, 1, 128))
            o_f, o_b = _ret_scan(p, ld, batch, seq)
            xs = _ret_out(o_f, o_b, p, xs, mod, ret_gn_w[j].reshape(1, RET_V),
                          ret_gn_b[j].reshape(1, RET_V), ret_w_out[j].astype(BF16), rows_all)
            xs = _ffn(xs, mod, ffn_gain, ffn_w_gate[j].astype(BF16), ffn_w_up[j].astype(BF16),
                      ffn_w_down[j].astype(BF16), rows_all)
        else:
            w_in = gla_w_in[j]
            wa = jnp.zeros((d, 128), F32).at[:, :2 * GLA_GATE_RANK].set(w_in[:, gla_main:])
            wbd = jnp.zeros((128, 2 * GLA_K), F32)
            wbd = wbd.at[:GLA_GATE_RANK, :GLA_K].set(gla_w_gate_up[j, 0])
            wbd = wbd.at[GLA_GATE_RANK:2 * GLA_GATE_RANK, GLA_K:].set(gla_w_gate_up[j, 1])
            p, lg = _gla_proj(xs, mod, mix_gain, w_in[:, :gla_main].astype(BF16), gla_colscale,
                              wa.astype(BF16), wbd.astype(BF16), gla_b_gate[j].reshape(1, 2 * GLA_K),
                              rows_all_proj)
            o_f, o_b = _gla_scan(p, lg, batch, seq)
            rows = rows_lat if last else rows_all
            base = rows.start if last else 0
            xs = _gla_out(o_f, o_b, p, xs, mod, gla_norm_g[j].reshape(1, GLA_V),
                          gla_w_out[j].astype(BF16), rows, base, rows.n * rows.tm)
            rows_c = _Rows(batch, seq, COMBINE_TM, last)
            base_c = rows_c.start if last else 0
            xs = _moe(xs, mod, ffn_gain, moe_w_router[j], moe_w_gate, moe_w_up, moe_w_down, j,
                      final_row, rows_c, base_c, last)
    return xs.reshape(batch, seq, d)
```

```python
import functools

import jax
import jax.numpy as jnp
from jax import lax
from jax.experimental import pallas as pl
from jax.experimental.pallas import tpu as pltpu

D_MODEL = 1024
GRID_W = 64
CTX_LEN = 256
DEPTH = 4
RET_HEADS = 4
RET_DK = 256
RET_DV = 512
RET_QK = 1024
RET_V = 2048
GLA_HEADS = 4
GLA_DK = 128
GLA_DV = 256
GLA_K = 512
GLA_V = 1024
GLA_GATE_RANK = 16
GLA_TAU = 16.0
D_FF = 2816
N_EXPERTS = 8
D_FF_EXPERT = 3584
ROPE_BASE = 10000.0
EPS = 1e-6

BF16 = jnp.bfloat16
F32 = jnp.float32

SCAN_ROWS = 256
GLA_SUB = 128
GLA_BLK = 16
VMEM_LIMIT = 48 * 1024 * 1024


def _cparams(sem):
    return pltpu.CompilerParams(dimension_semantics=sem, vmem_limit_bytes=VMEM_LIMIT)


def _dot(a, b):
    return jnp.dot(a, b, preferred_element_type=F32)


def _dot_nt(a, b):
    return lax.dot_general(a, b, (((1,), (1,)), ((), ())), preferred_element_type=F32)


def _dot_tn(a, b):
    return lax.dot_general(a, b, (((0,), (0,)), ((), ())), preferred_element_type=F32)


def _silu(x):
    return x * jax.nn.sigmoid(x)


def _norm_mod(x, gain, mod_ref, shift_i, scale_i):
    ms = jnp.mean(x * x, axis=-1, keepdims=True)
    y = x * lax.rsqrt(ms + EPS) * gain
    return y * (1.0 + mod_ref[scale_i]) + mod_ref[shift_i]


class _Rows:
    def __init__(self, batch, seq, tm, lat_only):
        self.tm = tm
        self.nct = batch * CTX_LEN // tm
        self.tpb = seq // tm
        self.start = self.nct if lat_only else 0
        self.n = batch * seq // tm + (0 if lat_only else self.nct)
        self.batch = batch

    def blk(self, base_tile=0, col=0):
        off = self.start - base_tile
        return lambda t, *_: (t + off, col)

    def mod_blk(self):
        start, nct, tpb, batch = self.start, self.nct, self.tpb, self.batch

        def im(t, *_):
            tg = t + start
            return (jnp.where(tg < nct, batch, (tg - nct) // tpb), 0, 0, 0)
        return im

    def pos_blk(self):
        start, nct, tpb = self.start, self.nct, self.tpb

        def im(t, *_):
            tg = t + start
            return (jnp.where(tg < nct, 0, 1 + (tg - nct) % tpb), 0)
        return im


def _mod_spec(rows):
    return pl.BlockSpec((None, 6, 1, D_MODEL), rows.mod_blk())


def _mod_kernel(c_ref, w_ref, b_ref, o_ref):
    s = _silu(c_ref[...]).astype(BF16)
    o_ref[...] = _dot(s, w_ref[...].astype(BF16)) + b_ref[...]


def _modulation(c, c_ctx, ada_w, ada_b):
    batch = c.shape[0]
    nrow = 16
    cc = jnp.concatenate([c, c_ctx[None], jnp.zeros((nrow - batch - 1, D_MODEL), F32)], axis=0)
    out = pl.pallas_call(
        _mod_kernel,
        out_shape=jax.ShapeDtypeStruct((DEPTH, 6, nrow, D_MODEL), F32),
        grid=(DEPTH, 6),
        in_specs=[
            pl.BlockSpec((nrow, D_MODEL), lambda i, k: (0, 0)),
            pl.BlockSpec((None, D_MODEL, D_MODEL), lambda i, k: (i, 0, k)),
            pl.BlockSpec((None, None, 1, D_MODEL), lambda i, k: (i, k, 0, 0)),
        ],
        out_specs=pl.BlockSpec((None, None, nrow, D_MODEL), lambda i, k: (i, k, 0, 0)),
        compiler_params=_cparams(("arbitrary", "arbitrary")),
        name="adaln_mod",
    )(cc, ada_w, ada_b.reshape(DEPTH, 6, 1, D_MODEL))
    return out.transpose(0, 2, 1, 3)[:, :, :, None, :]


PROJ_TN = 1024
PROJ_RC = 256


def _ret_proj_kernel(x_ref, mod_ref, gain_ref, w_ref, cs_ref, cos_ref, sin_ref, o_ref):
    rope_tiles = (0, (RET_QK + RET_V) // PROJ_TN)
    for r in range(x_ref.shape[0] // PROJ_RC):
        rs = slice(r * PROJ_RC, (r + 1) * PROJ_RC)
        h = _norm_mod(x_ref[rs, :], gain_ref[...], mod_ref, 0, 1).astype(BF16)
        for j in range(w_ref.shape[1] // PROJ_TN):
            cols = slice(j * PROJ_TN, (j + 1) * PROJ_TN)
            acc = _dot(h, w_ref[:, cols]) * cs_ref[:, cols]
            if j not in rope_tiles:
                o_ref[rs, cols] = acc.astype(BF16)
                continue
            for blk in range(PROJ_TN // 128):
                lo = blk * 128
                tl = (blk % 2) * 128
                a = acc[:, lo:lo + 128]
                rot = pltpu.roll(a, 64, 1)
                o_ref[rs, j * PROJ_TN + lo:j * PROJ_TN + lo + 128] = (
                    a * cos_ref[rs, tl:tl + 128] + rot * sin_ref[rs, tl:tl + 128]).astype(BF16)


def _rope_tables(seq, tm):
    half = RET_DK // 4
    freqs = ROPE_BASE ** (-jnp.arange(half, dtype=F32) / half)
    rows = seq // GRID_W
    row = jnp.broadcast_to(jnp.arange(rows, dtype=F32)[:, None], (rows, GRID_W)).reshape(seq)
    col = jnp.broadcast_to(jnp.arange(GRID_W, dtype=F32)[None, :], (rows, GRID_W)).reshape(seq)
    ar = row[:, None] * freqs[None, :]
    ac = col[:, None] * freqs[None, :]
    cos = jnp.concatenate([jnp.cos(ar), jnp.cos(ar), jnp.cos(ac), jnp.cos(ac)], axis=-1)
    sin = jnp.concatenate([-jnp.sin(ar), jnp.sin(ar), -jnp.sin(ac), jnp.sin(ac)], axis=-1)
    cos = jnp.concatenate([jnp.ones((tm, RET_DK), F32), cos], axis=0)
    sin = jnp.concatenate([jnp.zeros((tm, RET_DK), F32), sin], axis=0)
    return cos, sin


def _resident(shape):
    return pl.BlockSpec(shape, lambda *_: (0,) * len(shape), pipeline_mode=pl.Buffered(1))


def _ret_proj(xs, mod, gain, w_bf, colscale, cos, sin, rows):
    tm = rows.tm
    nout = w_bf.shape[1]
    return pl.pallas_call(
        _ret_proj_kernel,
        out_shape=jax.ShapeDtypeStruct((xs.shape[0], nout), BF16),
        grid=(rows.n,),
        in_specs=[
            pl.BlockSpec((tm, D_MODEL), rows.blk()),
            _mod_spec(rows),
            _resident((1, D_MODEL)),
            _resident((D_MODEL, nout)),
            _resident((1, nout)),
            pl.BlockSpec((tm, RET_DK), rows.pos_blk()),
            pl.BlockSpec((tm, RET_DK), rows.pos_blk()),
        ],
        out_specs=pl.BlockSpec((tm, nout), rows.blk()),
        compiler_params=_cparams(("arbitrary",)),
        name="ret_proj",
    )(xs, mod, gain, w_bf, colscale, cos, sin)


def _log_sigmoid(z):
    return jnp.minimum(z, 0.0) - jnp.log1p(jnp.exp(-jnp.abs(z)))


def _gla_proj_kernel(x_ref, mod_ref, gain_ref, w_ref, cs_ref, wa_ref, wbd_ref, bg_ref,
                     o_ref, lg_ref):
    for r in range(x_ref.shape[0] // PROJ_RC):
        rs = slice(r * PROJ_RC, (r + 1) * PROJ_RC)
        h = _norm_mod(x_ref[rs, :], gain_ref[...], mod_ref, 0, 1).astype(BF16)
        a = _dot(h, wa_ref[...]).astype(BF16)
        z = _dot(a, wbd_ref[...]) + bg_ref[...]
        lg_ref[rs, :] = _log_sigmoid(z) * (1.0 / GLA_TAU)
        for j in range(w_ref.shape[1] // PROJ_TN):
            cols = slice(j * PROJ_TN, (j + 1) * PROJ_TN)
            o_ref[rs, cols] = (_dot(h, w_ref[:, cols]) * cs_ref[:, cols]).astype(BF16)


def _gla_proj(xs, mod, gain, w_bf, colscale, wa_bf, wbd_bf, bgate, rows):
    tm = rows.tm
    nout = w_bf.shape[1]
    n = xs.shape[0]
    return pl.pallas_call(
        _gla_proj_kernel,
        out_shape=(jax.ShapeDtypeStruct((n, nout), BF16),
                   jax.ShapeDtypeStruct((n, 2 * GLA_K), F32)),
        grid=(rows.n,),
        in_specs=[
            pl.BlockSpec((tm, D_MODEL), rows.blk()),
            _mod_spec(rows),
            _resident((1, D_MODEL)),
            _resident((D_MODEL, nout)),
            _resident((1, nout)),
            _resident((D_MODEL, 128)),
            _resident((128, 2 * GLA_K)),
            _resident((1, 2 * GLA_K)),
        ],
        out_specs=(pl.BlockSpec((tm, nout), rows.blk()),
                   pl.BlockSpec((tm, 2 * GLA_K), rows.blk())),
        compiler_params=_cparams(("arbitrary",)),
        name="gla_proj",
    )(xs, mod, gain, w_bf, colscale, wa_bf, wbd_bf, bgate)


def _scan_row_maps(batch, seq):
    nlat = seq // SCAN_ROWS
    nctx_blocks = batch * CTX_LEN // SCAN_ROWS

    def fwd(b, c):
        return jnp.where(c == 0, b, nctx_blocks + b * nlat + c - 1)

    def bwd(b, c):
        return jnp.where(c == 0, b, nctx_blocks + b * nlat + nlat - c)
    return fwd, bwd, nlat + 1


def _ret_scan_kernel(ld_ref, qf_ref, kf_ref, vf_ref, qb_ref, kb_ref, vb_ref,
                     of_ref, ob_ref, sf_ref, sb_ref, dm_ref, qd_ref, kd_ref, cd_ref):
    b = pl.program_id(0)
    c = pl.program_id(1)
    n = SCAN_ROWS

    @pl.when(jnp.logical_and(b == 0, c == 0))
    def _():
        ii = lax.broadcasted_iota(jnp.int32, (n, n), 0).astype(F32)
        jj = lax.broadcasted_iota(jnp.int32, (n, n), 1).astype(F32)
        for d in range(2):
            rev = d == 1
            for h in range(RET_HEADS):
                lg = -jnp.exp(ld_ref[d * RET_HEADS + h])[:, :1]
                diff = (jj - ii) if rev else (ii - jj)
                dm_ref[d, h] = jnp.where(diff >= 0, jnp.exp(lg * jnp.maximum(diff, 0.0)), 0.0)
                qd_ref[d, h] = jnp.exp(lg * ((n - ii) if rev else (ii + 1.0)))
                kd_ref[d, h] = jnp.exp(lg * (ii if rev else (n - 1.0 - ii)))
                cd_ref[d, h] = jnp.broadcast_to(jnp.exp(lg * float(n)), (8, 128))

    @pl.when(c == 0)
    def _():
        sf_ref[...] = jnp.zeros_like(sf_ref)
        sb_ref[...] = jnp.zeros_like(sb_ref)

    def one(d, h, q_ref, k_ref, v_ref, o_ref, s_ref):
        qk = slice(h * RET_DK, (h + 1) * RET_DK)
        vv = slice(h * RET_DV, (h + 1) * RET_DV)
        q = q_ref[:, qk]
        k = k_ref[:, qk]
        v = v_ref[:, vv]
        scores = _dot_nt(q, k) * dm_ref[d, h]
        s = s_ref[h]
        o = _dot(scores.astype(BF16), v)
        o = o + _dot((q.astype(F32) * qd_ref[d, h]).astype(BF16), s.astype(BF16))
        o_ref[:, vv] = o.astype(o_ref.dtype)
        s_ref[h] = s * cd_ref[d, h][:1, :1] + _dot_tn((k.astype(F32) * kd_ref[d, h]).astype(BF16), v)

    for h in range(RET_HEADS):
        one(0, h, qf_ref, kf_ref, vf_ref, of_ref, sf_ref)
        one(1, h, qb_ref, kb_ref, vb_ref, ob_ref, sb_ref)


def _ret_scan(p, ld, batch, seq):
    fwd, bwd, nsteps = _scan_row_maps(batch, seq)
    n = p.shape[0]
    kcol = (RET_QK + RET_V) // RET_QK
    vcol = (2 * RET_QK + RET_V) // RET_V
    R = SCAN_ROWS

    def specs(rm):
        return [
            pl.BlockSpec((R, RET_QK), lambda b, c: (rm(b, c), 0)),
            pl.BlockSpec((R, RET_QK), lambda b, c: (rm(b, c), kcol)),
            pl.BlockSpec((R, RET_V), lambda b, c: (rm(b, c), vcol)),
        ]
    state = pltpu.VMEM((RET_HEADS, RET_DK, RET_DV), F32)
    table = pltpu.VMEM((2, RET_HEADS, R, R), F32)
    return pl.pallas_call(
        _ret_scan_kernel,
        out_shape=(jax.ShapeDtypeStruct((n, RET_V), BF16),
                   jax.ShapeDtypeStruct((n, RET_V), BF16)),
        grid=(batch, nsteps),
        in_specs=[pl.BlockSpec((2 * RET_HEADS, 1, 128), lambda b, c: (0, 0, 0))]
        + specs(fwd) + specs(bwd),
        out_specs=(pl.BlockSpec((R, RET_V), lambda b, c: (fwd(b, c), 0)),
                   pl.BlockSpec((R, RET_V), lambda b, c: (bwd(b, c), 0))),
        scratch_shapes=[state, state, table, table, table,
                        pltpu.VMEM((2, RET_HEADS, 8, 128), F32)],
        compiler_params=_cparams(("arbitrary", "arbitrary")),
        name="ret_scan",
    )(ld, p, p, p, p, p, p)


def _gla_tri(rev):
    n = GLA_SUB
    ri = lax.broadcasted_iota(jnp.int32, (n, n), 0)
    ci = lax.broadcasted_iota(jnp.int32, (n, n), 1)
    return (ci >= ri) if rev else (ci <= ri)


def _gla_cum_gates(g_ref, i, rev):
    n = GLA_SUB
    g = g_ref[i * n:(i + 1) * n, :]
    tmat = jnp.where(_gla_tri(rev), 1.0, 0.0).astype(BF16)
    g_hi = g.astype(BF16)
    g_lo = (g - g_hi.astype(F32)).astype(BF16)
    return _dot(tmat, g_hi) + _dot(tmat, g_lo)


def _gla_pairs(rev):
    nb = GLA_SUB // GLA_BLK
    return [(bj, bi) for bj in range(nb) for bi in (range(bj, nb) if rev else range(bj + 1))]


def _gla_operands(q_ref, k_ref, v_ref, b_all, h, i, rev):
    n, r = GLA_SUB, GLA_BLK
    nb = n // r
    rows = slice(i * n, (i + 1) * n)
    kcols = slice(h * GLA_DK, (h + 1) * GLA_DK)
    q = q_ref[rows, kcols].astype(F32)
    k = k_ref[rows, kcols].astype(F32)
    v = v_ref[rows, h * GLA_DV:(h + 1) * GLA_DV]
    b = b_all[:, kcols]
    b_last = b[0:1] if rev else b[n - 1:n]
    q_in = (q * jnp.exp(b)).astype(BF16)
    k_out = (k * jnp.exp(b_last - b)).astype(BF16)
    refs = [b[bi * r:bi * r + 1] if rev else b[bi * r + r - 1:bi * r + r] for bi in range(nb)]
    k_hat = jnp.concatenate(
        [k[bi * r:(bi + 1) * r] * jnp.exp(refs[bi] - b[bi * r:(bi + 1) * r]) for bi in range(nb)],
        axis=0).astype(BF16)
    q_cat = jnp.concatenate(
        [q[bj * r:(bj + 1) * r] * jnp.exp(b[bj * r:(bj + 1) * r] - refs[bi])
         for bj, bi in _gla_pairs(rev)], axis=0).astype(BF16)
    return dict(v=v, q_in=q_in, k_out=k_out, k_hat=k_hat, q_cat=q_cat, decay=jnp.exp(b_last))


def _gla_scores(raw, rev):
    n, r = GLA_SUB, GLA_BLK
    pairs = _gla_pairs(rev)
    col_blk = lax.broadcasted_iota(jnp.int32, (r, n), 1) // r
    row_blocks = []
    for bj in range(n // r):
        acc = jnp.zeros((r, n), F32)
        for gi, (pj, bi) in enumerate(pairs):
            if pj == bj:
                acc = jnp.where(col_blk == bi, raw[gi * r:(gi + 1) * r], acc)
        row_blocks.append(acc)
    return jnp.where(_gla_tri(rev), jnp.concatenate(row_blocks, axis=0), 0.0).astype(BF16)


def _gla_chunk_group(chains):
    ops = [_gla_operands(qr, kr, vr, b_all, h, i, rev) for (qr, kr, vr, b_all, _, h, i, rev) in chains]
    raws = [_dot_nt(o["q_cat"], o["k_hat"]) for o in ops]
    inter = [_dot_nt(o["q_in"], c[4].astype(BF16)) for o, c in zip(ops, chains)]
    kv = [_dot_tn(o["v"], o["k_out"]) for o in ops]
    scores = [_gla_scores(raw, c[7]) for raw, c in zip(raws, chains)]
    outs = [it + _dot(s, o["v"]) for it, s, o in zip(inter, scores, ops)]
    states = [c[4] * o["decay"] + x for c, o, x in zip(chains, ops, kv)]
    return outs, states


def _gla_scan_kernel(qf_ref, kf_ref, vf_ref, gf_ref, qb_ref, kb_ref, vb_ref, gb_ref,
                     of_ref, ob_ref, sf_ref, sb_ref):
    c = pl.program_id(1)
    nsub = SCAN_ROWS // GLA_SUB

    @pl.when(c == 0)
    def _():
        sf_ref[...] = jnp.zeros_like(sf_ref)
        sb_ref[...] = jnp.zeros_like(sb_ref)

    st_f = [sf_ref[h] for h in range(GLA_HEADS)]
    st_b = [sb_ref[h] for h in range(GLA_HEADS)]
    n = GLA_SUB
    for i in range(nsub):
        ib = nsub - 1 - i
        b_f = _gla_cum_gates(gf_ref, i, False)
        b_b = _gla_cum_gates(gb_ref, ib, True)
        chains = []
        for h in range(GLA_HEADS):
            chains.append((qf_ref, kf_ref, vf_ref, b_f, st_f[h], h, i, False))
            chains.append((qb_ref, kb_ref, vb_ref, b_b, st_b[h], h, ib, True))
        outs, states = _gla_chunk_group(chains)
        for h in range(GLA_HEADS):
            vcols = slice(h * GLA_DV, (h + 1) * GLA_DV)
            of_ref[i * n:(i + 1) * n, vcols] = outs[2 * h].astype(of_ref.dtype)
            ob_ref[ib * n:(ib + 1) * n, vcols] = outs[2 * h + 1].astype(ob_ref.dtype)
            st_f[h] = states[2 * h]
            st_b[h] = states[2 * h + 1]
    for h in range(GLA_HEADS):
        sf_ref[h] = st_f[h]
        sb_ref[h] = st_b[h]


def _gla_scan(p, lg, batch, seq):
    fwd, bwd, nsteps = _scan_row_maps(batch, seq)
    n = p.shape[0]
    kcol = (GLA_K + GLA_V) // GLA_K
    vcol = (2 * GLA_K + GLA_V) // GLA_V
    R = SCAN_ROWS

    def specs(rm, gcol):
        return [
            pl.BlockSpec((R, GLA_K), lambda b, c: (rm(b, c), 0)),
            pl.BlockSpec((R, GLA_K), lambda b, c: (rm(b, c), kcol)),
            pl.BlockSpec((R, GLA_V), lambda b, c: (rm(b, c), vcol)),
            pl.BlockSpec((R, GLA_K), lambda b, c: (rm(b, c), gcol)),
        ]
    state = pltpu.VMEM((GLA_HEADS, GLA_DV, GLA_DK), F32)
    return pl.pallas_call(
        _gla_scan_kernel,
        out_shape=(jax.ShapeDtypeStruct((n, GLA_V), BF16),
                   jax.ShapeDtypeStruct((n, GLA_V), BF16)),
        grid=(batch, nsteps),
        in_specs=specs(fwd, 0) + specs(bwd, 1),
        out_specs=(pl.BlockSpec((R, GLA_V), lambda b, c: (fwd(b, c), 0)),
                   pl.BlockSpec((R, GLA_V), lambda b, c: (bwd(b, c), 0))),
        scratch_shapes=[state, state],
        compiler_params=_cparams(("arbitrary", "arbitrary")),
        name="gla_scan",
    )(p, p, p, lg, p, p, p, lg)


def _ret_out_kernel(of_ref, ob_ref, g0_ref, g1_ref, x_ref, mod_ref, gnw_ref, gnb_ref, w_ref, o_ref):
    acc = jnp.zeros(o_ref.shape, F32)
    for h in range(RET_HEADS):
        sl = slice(h * RET_DV, (h + 1) * RET_DV)
        o = of_ref[:, sl].astype(F32) + ob_ref[:, sl].astype(F32)
        mu = jnp.mean(o, axis=-1, keepdims=True)
        d = o - mu
        var = jnp.mean(d * d, axis=-1, keepdims=True)
        y = d * lax.rsqrt(var + EPS) * gnw_ref[:, sl] + gnb_ref[:, sl]
        g_ref = g0_ref if h < 2 else g1_ref
        g = g_ref[:, (h % 2) * RET_DV:(h % 2 + 1) * RET_DV].astype(F32)
        acc = acc + _dot((y * _silu(g)).astype(BF16), w_ref[sl, :])
    o_ref[...] = x_ref[...] + mod_ref[2] * acc


def _ret_out(o_f, o_b, p, xs, mod, gn_w, gn_b, w_bf, rows):
    tm = rows.tm
    return pl.pallas_call(
        _ret_out_kernel,
        out_shape=jax.ShapeDtypeStruct(xs.shape, F32),
        grid=(rows.n,),
        in_specs=[
            pl.BlockSpec((tm, RET_V), rows.blk()),
            pl.BlockSpec((tm, RET_V), rows.blk()),
            pl.BlockSpec((tm, 1024), rows.blk(col=1)),
            pl.BlockSpec((tm, 1024), rows.blk(col=2)),
            pl.BlockSpec((tm, D_MODEL), rows.blk()),
            _mod_spec(rows),
            pl.BlockSpec((1, RET_V), lambda t: (0, 0)),
            pl.BlockSpec((1, RET_V), lambda t: (0, 0)),
            pl.BlockSpec((RET_V, D_MODEL), lambda t: (0, 0)),
        ],
        out_specs=pl.BlockSpec((tm, D_MODEL), rows.blk()),
        compiler_params=_cparams(("arbitrary",)),
        name="ret_out",
    )(o_f, o_b, p, p, xs, mod, gn_w, gn_b, w_bf)


def _gla_out_kernel(of_ref, ob_ref, r0_ref, r1_ref, x_ref, mod_ref, ng_ref, w_ref, o_ref):
    acc = jnp.zeros(o_ref.shape, F32)
    for h in range(GLA_HEADS):
        sl = slice(h * GLA_DV, (h + 1) * GLA_DV)
        o = of_ref[:, sl].astype(F32) + ob_ref[:, sl].astype(F32)
        y = o * lax.rsqrt(jnp.mean(o * o, axis=-1, keepdims=True) + EPS) * ng_ref[:, sl]
        r_ref = r0_ref if h < 2 else r1_ref
        r = r_ref[:, (h % 2) * GLA_DV:(h % 2 + 1) * GLA_DV].astype(F32)
        acc = acc + _dot((y * _silu(r)).astype(BF16), w_ref[sl, :])
    o_ref[...] = x_ref[...] + mod_ref[2] * acc


def _gla_out(o_f, o_b, p, xs, mod, norm_g, w_bf, rows, x_base_tile, out_rows):
    tm = rows.tm
    return pl.pallas_call(
        _gla_out_kernel,
        out_shape=jax.ShapeDtypeStruct((out_rows, D_MODEL), F32),
        grid=(rows.n,),
        in_specs=[
            pl.BlockSpec((tm, GLA_V), rows.blk()),
            pl.BlockSpec((tm, GLA_V), rows.blk()),
            pl.BlockSpec((tm, 512), rows.blk(col=1)),
            pl.BlockSpec((tm, 512), rows.blk(col=2)),
            pl.BlockSpec((tm, D_MODEL), rows.blk()),
            _mod_spec(rows),
            pl.BlockSpec((1, GLA_V), lambda t: (0, 0)),
            pl.BlockSpec((GLA_V, D_MODEL), lambda t: (0, 0)),
        ],
        out_specs=pl.BlockSpec((tm, D_MODEL), rows.blk(base_tile=x_base_tile)),
        compiler_params=_cparams(("arbitrary",)),
        name="gla_out",
    )(o_f, o_b, p, p, xs, mod, norm_g, w_bf)


def _ffn_kernel(x_ref, mod_ref, gain_ref, wg_ref, wu_ref, wd_ref, o_ref):
    for r in range(x_ref.shape[0] // PROJ_RC):
        rs = slice(r * PROJ_RC, (r + 1) * PROJ_RC)
        x = x_ref[rs, :]
        h = _norm_mod(x, gain_ref[...], mod_ref, 3, 4).astype(BF16)
        hid = (_silu(_dot(h, wg_ref[...])) * _dot(h, wu_ref[...])).astype(BF16)
        o_ref[rs, :] = x + mod_ref[5] * _dot(hid, wd_ref[...])


def _ffn(xs, mod, gain, wg_bf, wu_bf, wd_bf, rows):
    tm = rows.tm
    return pl.pallas_call(
        _ffn_kernel,
        out_shape=jax.ShapeDtypeStruct(xs.shape, F32),
        grid=(rows.n,),
        in_specs=[
            pl.BlockSpec((tm, D_MODEL), rows.blk()),
            _mod_spec(rows),
            _resident((1, D_MODEL)),
            _resident((D_MODEL, D_FF)),
            _resident((D_MODEL, D_FF)),
            _resident((D_FF, D_MODEL)),
        ],
        out_specs=pl.BlockSpec((tm, D_MODEL), rows.blk()),
        compiler_params=_cparams(("arbitrary",)),
        name="ffn",
    )(xs, mod, gain, wg_bf, wu_bf, wd_bf)


def _router_kernel(x_ref, mod_ref, gain_ref, wr_hi_ref, wr_lo_ref, h_ref, idx_ref, wt_ref):
    h = _norm_mod(x_ref[...], gain_ref[...], mod_ref, 3, 4)
    h_ref[...] = h
    h_hi = h.astype(BF16)
    h_lo = (h - h_hi.astype(F32)).astype(BF16)
    logits = _dot(h_hi, wr_hi_ref[...]) + _dot(h_lo, wr_hi_ref[...]) + _dot(h_hi, wr_lo_ref[...])
    lane = lax.broadcasted_iota(jnp.int32, logits.shape, 1)
    lane_f = lane.astype(F32)
    neg = jnp.float32(-jnp.inf)
    l1 = jnp.where(lane < N_EXPERTS, logits, neg)
    m1 = jnp.max(l1, axis=-1, keepdims=True)
    i1 = jnp.min(jnp.where(l1 == m1, lane_f, 128.0), axis=-1, keepdims=True)
    l2 = jnp.where(lane_f == i1, neg, l1)
    m2 = jnp.max(l2, axis=-1, keepdims=True)
    i2 = jnp.min(jnp.where(l2 == m2, lane_f, 128.0), axis=-1, keepdims=True)
    e2 = jnp.exp(m2 - m1)
    den = 1.0 + e2
    idx_ref[...] = jnp.where(lane == 0, i1, jnp.where(lane == 1, i2, 0.0)).astype(jnp.int32)
    wt_ref[...] = jnp.where(lane == 0, 1.0 / den, jnp.where(lane == 1, e2 / den, 0.0))


def _router(xs, mod, gain, wr_hi, wr_lo, rows, x_base_tile):
    tm = rows.tm
    nrow = rows.n * tm
    out_blk = lambda t: (t, 0)
    return pl.pallas_call(
        _router_kernel,
        out_shape=(jax.ShapeDtypeStruct((nrow, D_MODEL), F32),
                   jax.ShapeDtypeStruct((nrow, 128), jnp.int32),
                   jax.ShapeDtypeStruct((nrow, 128), F32)),
        grid=(rows.n,),
        in_specs=[
            pl.BlockSpec((tm, D_MODEL), rows.blk(base_tile=x_base_tile)),
            _mod_spec(rows),
            pl.BlockSpec((1, D_MODEL), lambda t: (0, 0)),
            pl.BlockSpec((D_MODEL, 128), lambda t: (0, 0)),
            pl.BlockSpec((D_MODEL, 128), lambda t: (0, 0)),
        ],
        out_specs=(pl.BlockSpec((tm, D_MODEL), out_blk),
                   pl.BlockSpec((tm, 128), out_blk),
                   pl.BlockSpec((tm, 128), out_blk)),
        compiler_params=_cparams(("arbitrary",)),
        name="moe_router",
    )(xs, mod, gain, wr_hi, wr_lo)


MOE_TM = 768
MOE_TF = 512


def _expert_kernel(te_ref, ta_ref, src_ref, tok_ref, h_ref, wg_ref, wu_ref, wd_ref, o_ref,
                   hbuf, hb_scr, acc_scr, sem):
    t = pl.program_id(0)
    f = pl.program_id(1)
    active = ta_ref[t] == 1

    nf = D_FF_EXPERT // MOE_TF
    per_step = MOE_TM // nf
    slot = t % 2
    last_sorted = tok_ref.shape[0] - 1

    def start_row(tile, sl, r):
        tok = tok_ref[jnp.minimum(src_ref[tile] + r, last_sorted)]
        pltpu.make_async_copy(h_ref.at[pl.ds(tok, 1)], hbuf.at[sl, pl.ds(r, 1)], sem.at[sl]).start()

    @pl.when(f == 0)
    def _():
        @pl.when(t == 0)
        def _():
            def body(r, carry):
                start_row(0, 0, r)
                return carry
            lax.fori_loop(0, MOE_TM, body, 0, unroll=8)

        requested = jnp.logical_or(t == 0, ta_ref[jnp.maximum(t - 1, 0)] == 1)

        @pl.when(requested)
        def _():
            pltpu.make_async_copy(h_ref.at[pl.ds(0, MOE_TM)], hbuf.at[slot], sem.at[slot]).wait()

        @pl.when(active)
        def _():
            hb_scr[...] = hbuf[slot].astype(BF16)
            for r in range(per_step * nf, MOE_TM):
                start_row(t + 1, 1 - slot, r)

        acc_scr[...] = jnp.zeros_like(acc_scr)

    @pl.when(active)
    def _():
        for r in range(per_step):
            start_row(t + 1, 1 - slot, f * per_step + r)
        h = hb_scr[...]
        gate = _dot(h, wg_ref[...].astype(BF16))
        up = _dot(h, wu_ref[...].astype(BF16))
        acc_scr[...] += _dot((_silu(gate) * up).astype(BF16), wd_ref[...].astype(BF16))

    @pl.when(f == pl.num_programs(1) - 1)
    def _():
        o_ref[...] = acc_scr[...]


def _experts(tile_expert, tile_active, tile_src, sorted_tok, h, wg_bf, wu_bf, wd_bf, layer):
    nslot = tile_expert.shape[0] * MOE_TM
    nf = D_FF_EXPERT // MOE_TF

    def fcol(t, f, ta):
        return jnp.where(ta[t] == 1, f, nf - 1)
    return pl.pallas_call(
        _expert_kernel,
        out_shape=jax.ShapeDtypeStruct((nslot, D_MODEL), F32),
        grid_spec=pltpu.PrefetchScalarGridSpec(
            num_scalar_prefetch=4,
            grid=(nslot // MOE_TM, nf),
            in_specs=[
                pl.BlockSpec(memory_space=pl.ANY),
                pl.BlockSpec((None, None, D_MODEL, MOE_TF),
                             lambda t, f, te, ta, ts, tk: (layer, te[t], 0, fcol(t, f, ta))),
                pl.BlockSpec((None, None, D_MODEL, MOE_TF),
                             lambda t, f, te, ta, ts, tk: (layer, te[t], 0, fcol(t, f, ta))),
                pl.BlockSpec((None, None, MOE_TF, D_MODEL),
                             lambda t, f, te, ta, ts, tk: (layer, te[t], fcol(t, f, ta), 0)),
            ],
            out_specs=pl.BlockSpec((MOE_TM, D_MODEL), lambda t, f, te, ta, ts, tk: (t, 0)),
            scratch_shapes=[pltpu.VMEM((2, MOE_TM, D_MODEL), F32),
                            pltpu.VMEM((MOE_TM, D_MODEL), BF16),
                            pltpu.VMEM((MOE_TM, D_MODEL), F32),
                            pltpu.SemaphoreType.DMA((2,))],
        ),
        compiler_params=_cparams(("arbitrary", "arbitrary")),
        name="moe_experts",
    )(tile_expert, tile_active, tile_src, sorted_tok, h, wg_bf, wu_bf, wd_bf)


COMBINE_TM = 256


def _combine_kernel(pos_ref, ys_ref, x_ref, wt_ref, mod_ref, fg_ref, o_ref, buf, sem, *, final):
    t = pl.program_id(0)
    nt = pl.num_programs(0)
    slot = t % 2

    def start_tile(tile, sl):
        base = tile * COMBINE_TM

        def body(r, carry):
            for k in range(2):
                p = pos_ref[2 * (base + r) + k]
                pltpu.make_async_copy(ys_ref.at[pl.ds(p, 1)], buf.at[sl, k, pl.ds(r, 1)],
                                      sem.at[sl]).start(priority=k)
            return carry
        lax.fori_loop(0, COMBINE_TM, body, 0, unroll=4)

    @pl.when(t == 0)
    def _():
        start_tile(0, 0)

    @pl.when(t + 1 < nt)
    def _():
        start_tile(t + 1, 1 - slot)

    for k in range(2):
        pltpu.make_async_copy(ys_ref.at[pl.ds(0, COMBINE_TM)], buf.at[slot, k], sem.at[slot]).wait()
    w = wt_ref[...]
    f = w[:, 0:1] * buf[slot, 0] + w[:, 1:2] * buf[slot, 1]
    y = x_ref[...] + mod_ref[5] * f
    if final:
        ms = jnp.mean(y * y, axis=-1, keepdims=True)
        y = y * lax.rsqrt(ms + EPS) * fg_ref[...]
    o_ref[...] = y


def _combine(pos, ys, xs, wts, mod, final_g, rows, x_base_tile, final):
    tm = rows.tm
    nrow = rows.n * tm
    return pl.pallas_call(
        functools.partial(_combine_kernel, final=final),
        out_shape=jax.ShapeDtypeStruct((nrow, D_MODEL), F32),
        grid_spec=pltpu.PrefetchScalarGridSpec(
            num_scalar_prefetch=1,
            grid=(rows.n,),
            in_specs=[
                pl.BlockSpec(memory_space=pl.ANY),
                pl.BlockSpec((tm, D_MODEL), rows.blk(base_tile=x_base_tile)),
                pl.BlockSpec((tm, 128), lambda t, p: (t, 0)),
                _mod_spec(rows),
                pl.BlockSpec((1, D_MODEL), lambda t, p: (0, 0)),
            ],
            out_specs=pl.BlockSpec((tm, D_MODEL), lambda t, p: (t, 0)),
            scratch_shapes=[pltpu.VMEM((2, 2, tm, D_MODEL), F32), pltpu.SemaphoreType.DMA((2,))],
        ),
        compiler_params=_cparams(("arbitrary",)),
        name="moe_combine",
    )(pos, ys, xs, wts, mod, final_g)


def _routing_tables(idx, nslot_pad):
    nrow = idx.shape[0]
    e_flat = idx[:, :2].reshape(-1)
    onehot = (e_flat[:, None] == jnp.arange(N_EXPERTS, dtype=jnp.int32)[None, :]).astype(jnp.int32)
    csum = jnp.cumsum(onehot, axis=0)
    rank = jnp.sum(csum * onehot, axis=1) - 1
    counts = csum[-1]
    padded = ((counts + MOE_TM - 1) // MOE_TM) * MOE_TM
    ends = jnp.cumsum(padded)
    offs = ends - padded
    pos = (jnp.sum(onehot * offs[None, :], axis=1) + rank).astype(jnp.int32)
    nslot = 2 * nrow
    assert nslot <= 1 << 16
    keys = jnp.sort(e_flat * (1 << 16) + jnp.arange(nslot, dtype=jnp.int32))
    sorted_tok = ((keys & 0xFFFF) >> 1).astype(jnp.int32)
    ntile = nslot_pad // MOE_TM
    tile_start = jnp.arange(ntile, dtype=jnp.int32) * MOE_TM
    tile_active = (tile_start < ends[-1]).astype(jnp.int32)
    last_start = jnp.maximum(ends[-1] - MOE_TM, 0)
    tile_start = jnp.minimum(tile_start, last_start)
    tile_expert = jnp.sum((tile_start[:, None] >= ends[None, :]).astype(jnp.int32), axis=1)
    tile_expert = jnp.minimum(tile_expert, N_EXPERTS - 1).astype(jnp.int32)
    te_onehot = (tile_expert[:, None] == jnp.arange(N_EXPERTS, dtype=jnp.int32)[None, :]).astype(jnp.int32)
    group_start = jnp.cumsum(counts) - counts
    tile_src = (jnp.sum(te_onehot * (group_start - offs)[None, :], axis=1) + tile_start).astype(jnp.int32)
    return pos, sorted_tok, tile_src, tile_expert, tile_active


def _moe(xs, mod, gain, wr, wg_bf, wu_bf, wd_bf, layer, final_g, rows, x_base_tile, final):
    wr_pad = jnp.zeros((D_MODEL, 128), F32).at[:, :N_EXPERTS].set(wr)
    wr_hi = wr_pad.astype(BF16)
    wr_lo = (wr_pad - wr_hi.astype(F32)).astype(BF16)
    h, idx, wts = _router(xs, mod, gain, wr_hi, wr_lo, rows, x_base_tile)
    nrow = h.shape[0]
    ntile = (2 * nrow + N_EXPERTS * (MOE_TM - 1)) // MOE_TM + 1
    nslot_pad = ntile * MOE_TM
    pos, sorted_tok, tile_src, tile_expert, tile_active = _routing_tables(idx, nslot_pad)
    ys = _experts(tile_expert, tile_active, tile_src, sorted_tok, h, wg_bf, wu_bf, wd_bf, layer)
    return _combine(pos, ys, xs, wts, mod, final_g, rows, x_base_tile, final)


def kernel(x, c, ctx, c_ctx, ada_w, ada_b, norm_mix_g, norm_ffn_g, final_g, ret_w_in, ret_log_decay,
           ret_gn_w, ret_gn_b, ret_w_out, gla_w_in, gla_w_gate_up, gla_b_gate, gla_norm_g, gla_w_out,
           ffn_w_gate, ffn_w_up, ffn_w_down, moe_w_router, moe_w_gate, moe_w_up, moe_w_down):
    batch, seq, d = x.shape
    assert d == D_MODEL and ctx.shape == (batch, CTX_LEN, d) and seq % 1024 == 0
    nctx = batch * CTX_LEN
    xs = jnp.concatenate([ctx.reshape(nctx, d), x.reshape(batch * seq, d)], axis=0)
    mod_all = _modulation(c, c_ctx, ada_w, ada_b)

    proj_tm = 512
    cos, sin = _rope_tables(seq, proj_tm)
    ret_colscale = jnp.concatenate([
        jnp.ones((1, RET_QK + RET_V), F32),
        jnp.full((1, RET_QK), RET_DK ** -0.5, F32),
        jnp.ones((1, RET_V), F32)], axis=1)
    gla_main = 2 * GLA_K + 2 * GLA_V
    gla_colscale = jnp.concatenate([
        jnp.full((1, GLA_K), GLA_DK ** -0.5, F32),
        jnp.ones((1, gla_main - GLA_K), F32)], axis=1)

    final_row = final_g.reshape(1, d)
    for i in range(DEPTH):
        last = i == DEPTH - 1
        j = i // 2
        mod = mod_all[i]
        rows_all_proj = _Rows(batch, seq, proj_tm, False)
        rows_all = _Rows(batch, seq, 512, False)
        rows_lat = _Rows(batch, seq, 512, True)
        mix_gain = norm_mix_g[i].reshape(1, d)
        ffn_gain = norm_ffn_g[i].reshape(1, d)
        if i % 2 == 0:
            p = _ret_proj(xs, mod, mix_gain, ret_w_in[j].astype(BF16), ret_colscale, cos, sin,
                          rows_all_proj)
            ld = jnp.broadcast_to(ret_log_decay[j].reshape(2 * RET_HEADS, 1, 1), (2 * RET_HEADS, 1, 128))
            o_f, o_b = _ret_scan(p, ld, batch, seq)
            xs = _ret_out(o_f, o_b, p, xs, mod, ret_gn_w[j].reshape(1, RET_V),
                          ret_gn_b[j].reshape(1, RET_V), ret_w_out[j].astype(BF16), rows_all)
            xs = _ffn(xs, mod, ffn_gain, ffn_w_gate[j].astype(BF16), ffn_w_up[j].astype(BF16),
                      ffn_w_down[j].astype(BF16), rows_all)
        else:
            w_in = gla_w_in[j]
            wa = jnp.zeros((d, 128), F32).at[:, :2 * GLA_GATE_RANK].set(w_in[:, gla_main:])
            wbd = jnp.zeros((128, 2 * GLA_K), F32)
            wbd = wbd.at[:GLA_GATE_RANK, :GLA_K].set(gla_w_gate_up[j, 0])
            wbd = wbd.at[GLA_GATE_RANK:2 * GLA_GATE_RANK, GLA_K:].set(gla_w_gate_up[j, 1])
            p, lg = _gla_proj(xs, mod, mix_gain, w_in[:, :gla_main].astype(BF16), gla_colscale,
                              wa.astype(BF16), wbd.astype(BF16), gla_b_gate[j].reshape(1, 2 * GLA_K),
                              rows_all_proj)
            o_f, o_b = _gla_scan(p, lg, batch, seq)
            rows = rows_lat if last else rows_all
            base = rows.start if last else 0
            xs = _gla_out(o_f, o_b, p, xs, mod, gla_norm_g[j].reshape(1, GLA_V),
                          gla_w_out[j].astype(BF16), rows, base, rows.n * rows.tm)
            rows_c = _Rows(batch, seq, COMBINE_TM, last)
            base_c = rows_c.start if last else 0
            xs = _moe(xs, mod, ffn_gain, moe_w_router[j], moe_w_gate, moe_w_up, moe_w_down, j,
                      final_row, rows_c, base_c, last)
    return xs.reshape(batch, seq, d)
```
